```python
import math
import jax, jax.numpy as jnp
from jax import lax
import numpy as np

D_MODEL = 4096
BATCH = 4
SEQ = 2048
DEPTH = 1
DEC_BATCH = 128
DEC_SEQ = 1
PAST_LEN = 16384
PAGE_SIZE = 128

HEAD_DIM = 128
N_MIX_HEADS = D_MODEL // HEAD_DIM
MLA_HEADS = N_MIX_HEADS // 2
RET_HEADS = N_MIX_HEADS - MLA_HEADS
Q_LORA = 1536
KV_LORA = 512
NOPE_DIM = 128
ROPE_DIM = 64
V_DIM = HEAD_DIM
RET_DK = HEAD_DIM
RET_DV = HEAD_DIM
RET_WIDTH = RET_HEADS * RET_DK
FFN_HIDDEN = -(-8 * D_MODEL // (3 * 256)) * 256
IN_COLS = Q_LORA + KV_LORA + ROPE_DIM + 4 * RET_WIDTH
MLA_SCALE = (NOPE_DIM + ROPE_DIM) ** -0.5
ROPE_BASE = 10000.0
RMS_EPS = 1e-6
GN_EPS = 1e-6
Q_BLOCK = 128
RET_CHUNK = 128
NEG_INF = -1e30

kernel_name = "hymba_mla_retention_decoder_step"

F32 = jnp.float32


def rms_norm(x, g):
    xf = x.astype(F32)
    y = xf * lax.rsqrt(jnp.mean(xf * xf, axis=-1, keepdims=True) + RMS_EPS)
    return (y * g.astype(F32)).astype(x.dtype)


def rope(x, pos):
    half = x.shape[-1] // 2
    inv = jnp.exp(-math.log(ROPE_BASE) * jnp.arange(half, dtype=F32) / half)
    ang = pos.astype(F32)[:, None] * inv[None, :]
    cos = jnp.cos(ang)[None, :, None, :]
    sin = jnp.sin(ang)[None, :, None, :]
    xf = x.astype(F32)
    x1, x2 = xf[..., :half], xf[..., half:]
    return jnp.concatenate([x1 * cos - x2 * sin, x2 * cos + x1 * sin], axis=-1).astype(x.dtype)


def retention_log_decay():
    return jnp.log1p(-jnp.exp2(-5.0 - jnp.arange(RET_HEADS, dtype=F32)))


def attention_inputs(x, pos, norm_attn, w_in, q_a_norm, w_q_up, kv_a_norm):
    B, L, _ = x.shape
    xn = rms_norm(x, norm_attn)
    h = xn @ w_in
    splits = list(np.cumsum([Q_LORA, KV_LORA, ROPE_DIM, RET_WIDTH, RET_WIDTH, RET_WIDTH]))
    q_a, c_kv, k_r, rq, rk, rv, rg = jnp.split(h, splits, axis=-1)
    q = (rms_norm(q_a, q_a_norm) @ w_q_up).reshape(B, L, MLA_HEADS, NOPE_DIM + ROPE_DIM)
    q_nope = q[..., :NOPE_DIM]
    q_rope = rope(q[..., NOPE_DIM:], pos)
    c = rms_norm(c_kv, kv_a_norm)
    kr = rope(k_r[:, :, None, :], pos)[:, :, 0, :]
    rq = rope(rq.reshape(B, L, RET_HEADS, RET_DK), pos)
    rk = rope(rk.reshape(B, L, RET_HEADS, RET_DK), pos) * (RET_DK ** -0.5)
    rv = rv.reshape(B, L, RET_HEADS, RET_DV)
    return q_nope, q_rope, c, kr, rq, rk, rv, rg


def mla_prompt(q_nope, q_rope, c, kr, pos, w_uk, w_uv):
    B, L = q_nope.shape[:2]
    qb = Q_BLOCK if L % Q_BLOCK == 0 else L
    n = L // qb
    qn = q_nope.reshape(B, n, qb, MLA_HEADS, NOPE_DIM).swapaxes(0, 1)
    qr = q_rope.reshape(B, n, qb, MLA_HEADS, ROPE_DIM).swapaxes(0, 1)
    qp = pos.reshape(n, qb)

    def block(args):
        qn_b, qr_b, qp_b = args
        q_lat = jnp.einsum('bqhd,chd->bqhc', qn_b, w_uk)
        s = (jnp.einsum('bqhc,bkc->bhqk', q_lat, c) + jnp.einsum('bqhr,bkr->bhqk', qr_b, kr)).astype(F32) * MLA_SCALE
        mask = pos[None, :] <= qp_b[:, None]
        s = jnp.where(mask[None, None], s, NEG_INF)
        p = jax.nn.softmax(s, axis=-1)
        o_lat = jnp.einsum('bhqk,bkc->bqhc', p.astype(c.dtype), c)
        return jnp.einsum('bqhc,che->bqhe', o_lat, w_uv)

    out = lax.map(block, (qn, qr, qp))
    return out.swapaxes(0, 1).reshape(B, L, MLA_HEADS * V_DIM)


def mla_sample(q_nope, q_rope, c_new, kr_new, cache_c, cache_r, page_table, w_uk, w_uv):
    B, T = q_nope.shape[:2]
    q_lat = jnp.einsum('bthd,chd->bthc', q_nope, w_uk).astype(F32)
    qr = q_rope.astype(F32)

    def scores(c_k, r_k):
        return (jnp.einsum('bthc,bkc->bthk', q_lat, c_k.astype(F32))
                + jnp.einsum('bthr,bkr->bthk', qr, r_k.astype(F32))) * MLA_SCALE

    def attend(carry, s, vals):
        m, l, acc = carry
        m_new = jnp.maximum(m, jnp.max(s, axis=-1))
        alpha = jnp.exp(m - m_new)
        p = jnp.exp(s - m_new[..., None])
        l = l * alpha + jnp.sum(p, axis=-1)
        acc = acc * alpha[..., None] + jnp.einsum('bthk,bkc->bthc', p, vals.astype(F32))
        return (m_new, l, acc)

    def page_step(carry, phys):
        c_pg = cache_c[phys]
        r_pg = cache_r[phys]
        return attend(carry, scores(c_pg, r_pg), c_pg), None

    init = (jnp.full((B, T, MLA_HEADS), NEG_INF, F32),
            jnp.zeros((B, T, MLA_HEADS), F32),
            jnp.zeros((B, T, MLA_HEADS, KV_LORA), F32))
    carry, _ = lax.scan(page_step, init, page_table.T)
    causal = jnp.tril(jnp.ones((T, T), dtype=bool))
    s_self = jnp.where(causal[None, :, None, :], scores(c_new, kr_new), NEG_INF)
    _, l, acc = attend(carry, s_self, c_new)
    o_lat = acc / l[..., None]
    out = jnp.einsum('bthc,che->bthe', o_lat, w_uv.astype(F32)).astype(q_nope.dtype)
    return out.reshape(B, T, MLA_HEADS * V_DIM)


def retention(q, k, v, s0):
    B, L = q.shape[:2]
    cs = RET_CHUNK if L % RET_CHUNK == 0 else L
    n = L // cs
    log_g = retention_log_decay()
    idx = jnp.arange(cs, dtype=F32)
    diff = idx[:, None] - idx[None, :]
    dec = jnp.where(diff[None] >= 0, jnp.exp(jnp.maximum(diff, 0.0)[None] * log_g[:, None, None]), 0.0)
    q_dec = jnp.exp((idx[:, None] + 1.0) * log_g[None, :])
    k_dec = jnp.exp((cs - 1.0 - idx)[:, None] * log_g[None, :])
    s_dec = jnp.exp(cs * log_g)

    def chunk(S, args):
        qc, kc, vc = args
        sc = jnp.einsum('blhd,bmhd->bhlm', qc, kc) * dec
        o = jnp.einsum('bhlm,bmhe->blhe', sc, vc)
        o = o + jnp.einsum('blhd,bhde->blhe', qc, S) * q_dec[None, :, :, None]
        S = S * s_dec[None, :, None, None] + jnp.einsum('bmhd,bmhe->bhde', kc * k_dec[None, :, :, None], vc)
        return S, o

    def to_chunks(a):
        return a.astype(F32).reshape(B, n, cs, RET_HEADS, a.shape[-1]).swapaxes(0, 1)

    S, o = lax.scan(chunk, s0.astype(F32), (to_chunks(q), to_chunks(k), to_chunks(v)))
    return o.swapaxes(0, 1).reshape(B, L, RET_HEADS, RET_DV), S


def retention_output(o, rg, gn_w):
    B, L = o.shape[:2]
    mu = jnp.mean(o, axis=-1, keepdims=True)
    var = jnp.mean(jnp.square(o - mu), axis=-1, keepdims=True)
    y = ((o - mu) * lax.rsqrt(var + GN_EPS)).reshape(B, L, RET_WIDTH) * gn_w.astype(F32)
    return (jax.nn.silu(rg.astype(F32)) * y).astype(rg.dtype)


def mix_and_ffn(x, mla_o, ret_o, rg, ret_gn_w, w_o, norm_ffn, w_gate, w_up, w_down):
    mix = jnp.concatenate([mla_o, retention_output(ret_o, rg, ret_gn_w)], axis=-1)
    h = x + mix @ w_o
    hn = rms_norm(h, norm_ffn)
    return h + (jax.nn.silu(hn @ w_gate) * (hn @ w_up)) @ w_down


def setup_inputs(seed: int = 0) -> dict:
    key = jax.random.key(seed)
    ks = jax.random.split(key, 24)
    n_pages = PAST_LEN // PAGE_SIZE
    n_used = DEC_BATCH * n_pages
    n_pool = n_used + n_used // 4
    perm = jax.random.permutation(ks[0], n_pool).astype(jnp.int32)
    page_table = perm[:n_used].reshape(DEC_BATCH, n_pages)

    def nrm(k, shape, scale):
        return jax.random.normal(k, shape, F32) * scale

    def gain(k, shape):
        return 1.0 + 0.01 * jax.random.normal(k, shape, F32)

    return {
        "x_prompt": nrm(ks[1], (BATCH, SEQ, D_MODEL), 1.0),
        "x_sample": nrm(ks[2], (DEC_BATCH, DEC_SEQ, D_MODEL), 1.0),
        "cache_kv_latent": nrm(ks[3], (DEPTH, n_pool, PAGE_SIZE, KV_LORA), 1.0),
        "cache_k_rope": nrm(ks[4], (DEPTH, n_pool, PAGE_SIZE, ROPE_DIM), 1.0),
        "state_retention": nrm(ks[5], (DEPTH, DEC_BATCH, RET_HEADS, RET_DK, RET_DV), 0.1),
        "page_table": page_table,
        "w_in": nrm(ks[6], (DEPTH, D_MODEL, IN_COLS), D_MODEL ** -0.5),
        "norm_attn": gain(ks[7], (DEPTH, D_MODEL)),
        "q_a_norm": gain(ks[8], (DEPTH, Q_LORA)),
        "w_q_up": nrm(ks[9], (DEPTH, Q_LORA, MLA_HEADS * (NOPE_DIM + ROPE_DIM)), Q_LORA ** -0.5),
        "kv_a_norm": gain(ks[10], (DEPTH, KV_LORA)),
        "w_uk": nrm(ks[11], (DEPTH, KV_LORA, MLA_HEADS, NOPE_DIM), KV_LORA ** -0.5),
        "w_uv": nrm(ks[12], (DEPTH, KV_LORA, MLA_HEADS, V_DIM), KV_LORA ** -0.5),
        "ret_gn_w": gain(ks[13], (DEPTH, RET_WIDTH)),
        "w_o": nrm(ks[14], (DEPTH, D_MODEL, D_MODEL), D_MODEL ** -0.5),
        "norm_ffn": gain(ks[15], (DEPTH, D_MODEL)),
        "w_gate": nrm(ks[16], (DEPTH, D_MODEL, FFN_HIDDEN), D_MODEL ** -0.5),
        "w_up": nrm(ks[17], (DEPTH, D_MODEL, FFN_HIDDEN), D_MODEL ** -0.5),
        "w_down": nrm(ks[18], (DEPTH, FFN_HIDDEN, D_MODEL), FFN_HIDDEN ** -0.5),
        "norm_final": gain(ks[19], (D_MODEL,)),
    }


def reference(x_prompt, x_sample, cache_kv_latent, cache_k_rope, state_retention, page_table,
              w_in, norm_attn, q_a_norm, w_q_up, kv_a_norm, w_uk, w_uv, ret_gn_w, w_o,
              norm_ffn, w_gate, w_up, w_down, norm_final):
    B, L = x_prompt.shape[:2]
    T = x_sample.shape[1]
    past = page_table.shape[1] * PAGE_SIZE
    pos_p = jnp.arange(L, dtype=jnp.int32)
    pos_s = past + jnp.arange(T, dtype=jnp.int32)
    xp, xs = x_prompt, x_sample
    lat_p, kr_p, st_p, lat_s, kr_s, st_s = [], [], [], [], [], []
    for i in range(DEPTH):
        qn, qr, c, kr, rq, rk, rv, rg = attention_inputs(xp, pos_p, norm_attn[i], w_in[i], q_a_norm[i], w_q_up[i], kv_a_norm[i])
        mla_o = mla_prompt(qn, qr, c, kr, pos_p, w_uk[i], w_uv[i])
        ret_o, S = retention(rq, rk, rv, jnp.zeros((B, RET_HEADS, RET_DK, RET_DV), F32))
        xp = mix_and_ffn(xp, mla_o, ret_o, rg, ret_gn_w[i], w_o[i], norm_ffn[i], w_gate[i], w_up[i], w_down[i])
        lat_p.append(c)
        kr_p.append(kr)
        st_p.append(S.astype(x_prompt.dtype))
        qn, qr, c, kr, rq, rk, rv, rg = attention_inputs(xs, pos_s, norm_attn[i], w_in[i], q_a_norm[i], w_q_up[i], kv_a_norm[i])
        mla_o = mla_sample(qn, qr, c, kr, cache_kv_latent[i], cache_k_rope[i], page_table, w_uk[i], w_uv[i])
        ret_o, S = retention(rq, rk, rv, state_retention[i])
        xs = mix_and_ffn(xs, mla_o, ret_o, rg, ret_gn_w[i], w_o[i], norm_ffn[i], w_gate[i], w_up[i], w_down[i])
        lat_s.append(c)
        kr_s.append(kr)
        st_s.append(S.astype(state_retention.dtype))
    y_prompt = rms_norm(xp, norm_final)
    y_sample = rms_norm(xs, norm_final)
    return (y_prompt, y_sample, jnp.stack(lat_p), jnp.stack(kr_p), jnp.stack(st_p),
            jnp.stack(lat_s), jnp.stack(kr_s), jnp.stack(st_s))
```

```python
import functools
import math

import numpy as np
import jax
import jax.numpy as jnp
from jax import lax
from jax.experimental import pallas as pl
from jax.experimental.pallas import tpu as pltpu

F32 = jnp.float32
BF16 = jnp.bfloat16

HEAD_DIM = 128
MLA_HEADS = 16
RET_HEADS = 16
Q_LORA = 1536
KV_LORA = 512
NOPE_DIM = 128
ROPE_DIM = 64
RET_WIDTH = RET_HEADS * HEAD_DIM
QK_PAD = 256
MAIN_COLS = Q_LORA + KV_LORA + 4 * RET_WIDTH
MLA_SCALE = (NOPE_DIM + ROPE_DIM) ** -0.5
ROPE_BASE = 10000.0
RMS_EPS = 1e-6
GN_EPS = 1e-6
NEG_INF = -1e30
PAGE_SIZE = 128
FFN_PAD = 1024

VMEM_LIMIT = 56 * 1024 * 1024


def _params(*sem):
    return pltpu.CompilerParams(dimension_semantics=sem, vmem_limit_bytes=VMEM_LIMIT)


def _rms(x, g):
    ms = jnp.mean(x * x, axis=-1, keepdims=True)
    return x * lax.rsqrt(ms + RMS_EPS) * g


def _silu(x):
    return x * (1.0 / (1.0 + jnp.exp(-x)))


def _dot(a, b):
    return jnp.dot(a, b, preferred_element_type=F32)


def _dot_nt(a, b):
    return lax.dot_general(a, b, (((1,), (1,)), ((), ())), preferred_element_type=F32)


def _dot_tn(a, b):
    return lax.dot_general(a, b, (((0,), (0,)), ((), ())), preferred_element_type=F32)


def _rmsnorm_kernel(x_ref, g_ref, o_ref):
    o_ref[...] = _rms(x_ref[...], g_ref[...]).astype(o_ref.dtype)


def rmsnorm(x, g, out_dtype, br=256):
    R, D = x.shape
    br = min(br, R)
    return pl.pallas_call(
        _rmsnorm_kernel,
        grid=(R // br,),
        in_specs=[pl.BlockSpec((br, D), lambda i: (i, 0)),
                  pl.BlockSpec((1, D), lambda i: (0, 0))],
        out_specs=pl.BlockSpec((br, D), lambda i: (i, 0)),
        out_shape=jax.ShapeDtypeStruct((R, D), out_dtype),
        compiler_params=_params("parallel"),
        name="rmsnorm",
    )(x, g.reshape(1, D))


def _mm_kernel(*refs, nk, mode):
    if mode == "glu":
        x_ref, w_ref, w2_ref, o_ref = refs
        x = x_ref[...]
        g = _dot(x, w_ref[...])
        u = _dot(x, w2_ref[...])
        o_ref[...] = (_silu(g) * u).astype(o_ref.dtype)
        return
    if mode == "res":
        x_ref, w_ref, r_ref, o_ref = refs[:4]
    else:
        x_ref, w_ref, o_ref = refs[:3]
        r_ref = None

    def finish(acc):
        if r_ref is not None:
            acc = r_ref[...] + acc
        o_ref[...] = acc.astype(o_ref.dtype)

    if nk == 1:
        finish(_dot(x_ref[...], w_ref[...]))
        return
    acc_ref = refs[-1]
    k = pl.program_id(2)

    @pl.when(k == 0)
    def _():
        acc_ref[...] = _dot(x_ref[...], w_ref[...])

    @pl.when(k > 0)
    def _():
        acc_ref[...] += _dot(x_ref[...], w_ref[...])

    @pl.when(k == nk - 1)
    def _():
        finish(acc_ref[...])


def matmul(x, w, *, bm, bn, bk=None, out_dtype=F32, res=None, w2=None, name="matmul"):
    M, K = x.shape
    N = w.shape[1]
    bm = min(bm, M)
    bn = min(bn, N)
    bk = K if bk is None else bk
    nk = K // bk
    mode = "glu" if w2 is not None else ("res" if res is not None else "plain")
    assert M % bm == 0 and N % bn == 0 and K % bk == 0
    assert not (mode == "glu" and nk != 1)
    in_specs = [pl.BlockSpec((bm, bk), lambda i, j, k: (i, k)),
                pl.BlockSpec((bk, bn), lambda i, j, k: (k, j))]
    args = [x, w]
    if mode == "glu":
        in_specs.append(pl.BlockSpec((bk, bn), lambda i, j, k: (k, j)))
        args.append(w2)
    if mode == "res":
        in_specs.append(pl.BlockSpec((bm, bn), lambda i, j, k: (i, j)))
        args.append(res)
    scratch = [pltpu.VMEM((bm, bn), F32)] if nk > 1 else []
    return pl.pallas_call(
        functools.partial(_mm_kernel, nk=nk, mode=mode),
        grid=(M // bm, N // bn, nk),
        in_specs=in_specs,
        out_specs=pl.BlockSpec((bm, bn), lambda i, j, k: (i, j)),
        out_shape=jax.ShapeDtypeStruct((M, N), out_dtype),
        scratch_shapes=scratch,
        compiler_params=_params("parallel", "parallel", "arbitrary"),
        name=name,
    )(*args)


def _mm2_kernel(xa_ref, xb_ref, wa_ref, wb_ref, r_ref, o_ref):
    acc = _dot(xa_ref[...], wa_ref[...]) + _dot(xb_ref[...], wb_ref[...])
    o_ref[...] = r_ref[...] + acc


def out_proj(xa, xb, w, res, *, bm, bn):
    M, Ka = xa.shape
    Kb = xb.shape[1]
    N = w.shape[1]
    bm = min(bm, M)
    assert Ka == Kb and M % bm == 0 and N % bn == 0
    return pl.pallas_call(
        _mm2_kernel,
        grid=(M // bm, N // bn),
        in_specs=[pl.BlockSpec((bm, Ka), lambda i, j: (i, 0)),
                  pl.BlockSpec((bm, Kb), lambda i, j: (i, 0)),
                  pl.BlockSpec((Ka, bn), lambda i, j: (0, j)),
                  pl.BlockSpec((Kb, bn), lambda i, j: (1, j)),
                  pl.BlockSpec((bm, bn), lambda i, j: (i, j))],
        out_specs=pl.BlockSpec((bm, bn), lambda i, j: (i, j)),
        out_shape=jax.ShapeDtypeStruct((M, N), F32),
        compiler_params=_params("parallel", "parallel"),
        name="out_proj",
    )(xa, xb, w, w, res)


def _rope_tables(pos, half, width):
    inv = jnp.exp(-math.log(ROPE_BASE) * jnp.arange(half, dtype=F32) / half)
    ang = pos.astype(F32)[:, None] * inv[None, :]
    cos, sin = jnp.cos(ang), jnp.sin(ang)
    n = pos.shape[0]
    pad = width - 2 * half
    cos_t = jnp.concatenate([cos, cos, jnp.ones((n, pad), F32)], axis=-1)
    sin_t = jnp.concatenate([-sin, sin, jnp.zeros((n, pad), F32)], axis=-1)
    return cos_t, sin_t


def _rope64_tile(t, cos, sin):
    lane = lax.broadcasted_iota(jnp.int32, t.shape, 1)
    other = jnp.where(lane < ROPE_DIM // 2, pltpu.roll(t, 128 - ROPE_DIM // 2, axis=1),
                      pltpu.roll(t, ROPE_DIM // 2, axis=1))
    return t * cos + other * sin


def _rope128_tile(t, cos, sin):
    return t * cos + pltpu.roll(t, HEAD_DIM // 2, axis=1) * sin


def _mla_prep_kernel(h_ref, kr_ref, cos_ref, sin_ref, gq_ref, gkv_ref, wq_ref, *rest, want_kv):
    if want_kv:
        wuk_ref, wuv_ref, q_ref, c_ref, kro_ref, k_ref, v_ref = rest
    else:
        q_ref, c_ref, kro_ref = rest
    cos, sin = cos_ref[...], sin_ref[...]
    qn = _rms(h_ref[:, :Q_LORA], gq_ref[...]).astype(BF16)
    q = _dot(qn, wq_ref[...])
    for hh in range(MLA_HEADS):
        o = hh * QK_PAD
        q_ref[:, o:o + NOPE_DIM] = q[:, o:o + NOPE_DIM].astype(BF16)
        q_ref[:, o + NOPE_DIM:o + QK_PAD] = _rope64_tile(q[:, o + NOPE_DIM:o + QK_PAD], cos, sin).astype(BF16)
    c = _rms(h_ref[:, Q_LORA:Q_LORA + KV_LORA], gkv_ref[...])
    c_ref[...] = c
    kr = _rope64_tile(kr_ref[...], cos, sin)
    kro_ref[...] = kr[:, :ROPE_DIM]
    if want_kv:
        cb = c.astype(BF16)
        krb = kr.astype(BF16)
        kn = _dot(cb, wuk_ref[...])
        for hh in range(MLA_HEADS):
            o = hh * QK_PAD
            k_ref[:, o:o + NOPE_DIM] = kn[:, hh * NOPE_DIM:(hh + 1) * NOPE_DIM].astype(BF16)
            k_ref[:, o + NOPE_DIM:o + QK_PAD] = krb
        v_ref[...] = _dot(cb, wuv_ref[...]).astype(BF16)


def mla_prep(h_main, kr_raw, cos_t, sin_t, gq, gkv, wq_pad, wuk2=None, wuv2=None, *, bm=256):
    R = h_main.shape[0]
    bm = min(bm, R)
    ntab = cos_t.shape[0] // bm
    want_kv = wuk2 is not None
    row = lambda i: (i, 0)
    const = lambda i: (0, 0)
    in_specs = [pl.BlockSpec((bm, Q_LORA + KV_LORA), row),
                pl.BlockSpec((bm, 128), row),
                pl.BlockSpec((bm, 128), lambda i: (i % ntab, 0)),
                pl.BlockSpec((bm, 128), lambda i: (i % ntab, 0)),
                pl.BlockSpec((1, Q_LORA), const),
                pl.BlockSpec((1, KV_LORA), const),
                pl.BlockSpec(wq_pad.shape, const, pipeline_mode=pl.Buffered(1))]
    args = [h_main, kr_raw, cos_t, sin_t, gq.reshape(1, -1), gkv.reshape(1, -1), wq_pad]
    out_specs = [pl.BlockSpec((bm, MLA_HEADS * QK_PAD), row),
                 pl.BlockSpec((bm, KV_LORA), row),
                 pl.BlockSpec((bm, ROPE_DIM), row)]
    out_shape = [jax.ShapeDtypeStruct((R, MLA_HEADS * QK_PAD), BF16),
                 jax.ShapeDtypeStruct((R, KV_LORA), F32),
                 jax.ShapeDtypeStruct((R, ROPE_DIM), F32)]
    if want_kv:
        in_specs += [pl.BlockSpec(wuk2.shape, const), pl.BlockSpec(wuv2.shape, const)]
        args += [wuk2, wuv2]
        out_specs += [pl.BlockSpec((bm, MLA_HEADS * QK_PAD), row),
                      pl.BlockSpec((bm, MLA_HEADS * HEAD_DIM), row)]
        out_shape += [jax.ShapeDtypeStruct((R, MLA_HEADS * QK_PAD), BF16),
                      jax.ShapeDtypeStruct((R, MLA_HEADS * HEAD_DIM), BF16)]
    return pl.pallas_call(
        functools.partial(_mla_prep_kernel, want_kv=want_kv),
        grid=(R // bm,),
        in_specs=in_specs, out_specs=out_specs, out_shape=out_shape,
        compiler_params=_params("parallel"),
        name="mla_prep",
    )(*args)


def _flash_kernel(q_ref, k_ref, v_ref, o_ref, *, L, bq):
    def step(q, k, v, carry, mask):
        m, l, acc = carry
        s = _dot_nt(q, k) * MLA_SCALE
        if mask is not None:
            s = jnp.where(mask, s, NEG_INF)
        m_new = jnp.maximum(m, jnp.max(s, axis=-1, keepdims=True))
        alpha = jnp.exp(m - m_new)
        p = jnp.exp(s - m_new)
        l = l * alpha + jnp.sum(p, axis=-1, keepdims=True)
        acc = acc * alpha + _dot(p.astype(BF16), v)
        return m_new, l, acc

    tri = (lax.broadcasted_iota(jnp.int32, (bq, bq), 0) >= lax.broadcasted_iota(jnp.int32, (bq, bq), 1))
    for qi in range(L // bq):
        q = q_ref[qi * bq:(qi + 1) * bq, :]
        carry = (jnp.full((bq, 1), NEG_INF, F32), jnp.zeros((bq, 1), F32), jnp.zeros((bq, HEAD_DIM), F32))

        def body(j, carry, q=q):
            off = pl.multiple_of(j * bq, bq)
            return step(q, k_ref[pl.ds(off, bq), :], v_ref[pl.ds(off, bq), :], carry, None)

        if qi > 0:
            carry = lax.fori_loop(0, qi, body, carry)
        m, l, acc = step(q, k_ref[qi * bq:(qi + 1) * bq, :], v_ref[qi * bq:(qi + 1) * bq, :], carry, tri)
        o_ref[qi * bq:(qi + 1) * bq, :] = (acc / l).astype(o_ref.dtype)


def mla_prompt_attention(Q, K, V, B, L, *, bq=256):
    bq = min(bq, L)
    return pl.pallas_call(
        functools.partial(_flash_kernel, L=L, bq=bq),
        grid=(B, MLA_HEADS),
        in_specs=[pl.BlockSpec((L, QK_PAD), lambda b, h: (b, h)),
                  pl.BlockSpec((L, QK_PAD), lambda b, h: (b, h)),
                  pl.BlockSpec((L, HEAD_DIM), lambda b, h: (b, h))],
        out_specs=pl.BlockSpec((L, HEAD_DIM), lambda b, h: (b, h)),
        out_shape=jax.ShapeDtypeStruct((B * L, MLA_HEADS * HEAD_DIM), BF16),
        compiler_params=_params("parallel", "parallel"),
        name="mla_prompt_attention",
    )(Q, K, V)


def _group_norm_gate(o, rg, gw):
    mu = jnp.mean(o, axis=-1, keepdims=True)
    d = o - mu
    var = jnp.mean(d * d, axis=-1, keepdims=True)
    return _silu(rg) * (d * lax.rsqrt(var + GN_EPS) * gw)


def _ret_prompt_kernel(rq_ref, rk_ref, rv_ref, rg_ref, cos_ref, sin_ref, gw_ref, lg_ref,
                       o_ref, s_ref, *, L, C):
    lg = lg_ref[0, :, :1]
    ii = lax.broadcasted_iota(jnp.int32, (C, C), 0)
    jj = lax.broadcasted_iota(jnp.int32, (C, C), 1)
    diff = (ii - jj).astype(F32)
    dec = jnp.where(diff >= 0, jnp.exp(jnp.maximum(diff, 0.0) * lg), 0.0)
    idx = lax.broadcasted_iota(jnp.int32, (C, 1), 0).astype(F32)
    q_dec = jnp.exp((idx + 1.0) * lg)
    k_dec = jnp.exp((C - 1.0 - idx) * lg)
    s_dec = jnp.exp(C * lg)
    gw = gw_ref[...]
    S = jnp.zeros((HEAD_DIM, HEAD_DIM), F32)
    for n in range(L // C):
        r = slice(n * C, (n + 1) * C)
        cos, sin = cos_ref[r, :], sin_ref[r, :]
        qc = _rope128_tile(rq_ref[r, :], cos, sin)
        kc = _rope128_tile(rk_ref[r, :], cos, sin) * (HEAD_DIM ** -0.5)
        qb = qc.astype(BF16)
        vb = rv_ref[r, :].astype(BF16)
        sc = _dot_nt(qb, kc.astype(BF16)) * dec
        o = _dot(sc.astype(BF16), vb) + _dot(qb, S.astype(BF16)) * q_dec
        S = S * s_dec + _dot_tn((kc * k_dec).astype(BF16), vb)
        o_ref[r, :] = _group_norm_gate(o, rg_ref[r, :], gw).astype(o_ref.dtype)
    s_ref[...] = S


def retention_prompt(h_main, cos_t, sin_t, gn_w, lg_tab, B, L, *, C=256):
    C = min(C, L)
    nb = RET_WIDTH // HEAD_DIM
    base = (Q_LORA + KV_LORA) // HEAD_DIM
    blk = lambda off: pl.BlockSpec((L, HEAD_DIM), lambda b, h, off=off: (b, off + h))
    return pl.pallas_call(
        functools.partial(_ret_prompt_kernel, L=L, C=C),
        grid=(B, RET_HEADS),
        in_specs=[blk(base), blk(base + nb), blk(base + 2 * nb), blk(base + 3 * nb),
                  pl.BlockSpec((L, HEAD_DIM), lambda b, h: (0, 0)),
                  pl.BlockSpec((L, HEAD_DIM), lambda b, h: (0, 0)),
                  pl.BlockSpec((1, HEAD_DIM), lambda b, h: (0, h)),
                  pl.BlockSpec((1, 1, HEAD_DIM), lambda b, h: (h, 0, 0))],
        out_specs=[pl.BlockSpec((L, HEAD_DIM), lambda b, h: (b, h)),
                   pl.BlockSpec((None, None, HEAD_DIM, HEAD_DIM), lambda b, h: (b, h, 0, 0))],
        out_shape=[jax.ShapeDtypeStruct((B * L, RET_WIDTH), BF16),
                   jax.ShapeDtypeStruct((B, RET_HEADS, HEAD_DIM, HEAD_DIM), F32)],
        compiler_params=_params("parallel", "parallel"),
        name="retention_prompt",
    )(h_main, h_main, h_main, h_main, cos_t, sin_t, gn_w.reshape(1, -1), lg_tab)


def _ret_sample_kernel(rq_ref, rk_ref, rv_ref, rg_ref, cos_ref, sin_ref, gw_ref, lg_ref, s0_ref,
                       o_ref, s_ref, q_scr, k_scr, v_scr, oc_scr, *, bb):
    gamma = jnp.exp(lg_ref[0, :, :1])
    cos, sin = cos_ref[...], sin_ref[...]
    q = _rope128_tile(rq_ref[...], cos, sin)
    k = _rope128_tile(rk_ref[...], cos, sin) * (HEAD_DIM ** -0.5)
    v = rv_ref[...]
    q_scr[...] = q
    k_scr[...] = k
    v_scr[...] = v
    eye = (lax.broadcasted_iota(jnp.int32, (HEAD_DIM, HEAD_DIM), 0)
           == lax.broadcasted_iota(jnp.int32, (HEAD_DIM, HEAD_DIM), 1))

    def body(b, carry):
        s0 = s0_ref[b]
        qb = jnp.broadcast_to(q_scr[pl.ds(b, 1), :], (8, HEAD_DIM)).astype(BF16)
        oc_scr[pl.ds(b, 1), :] = _dot(qb, s0.astype(BF16))[:1, :]
        kd = jnp.where(eye, jnp.broadcast_to(k_scr[pl.ds(b, 1), :], (HEAD_DIM, HEAD_DIM)), 0.0)
        vr = jnp.broadcast_to(v_scr[pl.ds(b, 1), :], (HEAD_DIM, HEAD_DIM))
        s_ref[b] = s0 * gamma + _dot(kd.astype(BF16), vr.astype(BF16))
        return carry

    lax.fori_loop(0, bb, body, 0)
    qk = jnp.sum(q * k, axis=-1, keepdims=True)
    o = qk * v + oc_scr[...] * gamma
    o_ref[...] = _group_norm_gate(o, rg_ref[...], gw_ref[...]).astype(o_ref.dtype)


def retention_sample(h_main, cos_t, sin_t, gn_w, lg_tab, s0, *, bb=32):
    Bs = h_main.shape[0]
    bb = min(bb, Bs)
    nb = RET_WIDTH // HEAD_DIM
    base = (Q_LORA + KV_LORA) // HEAD_DIM
    blk = lambda off: pl.BlockSpec((bb, HEAD_DIM), lambda h, i, off=off: (i, off + h))
    return pl.pallas_call(
        functools.partial(_ret_sample_kernel, bb=bb),
        grid=(RET_HEADS, Bs // bb),
        in_specs=[blk(base), blk(base + nb), blk(base + 2 * nb), blk(base + 3 * nb),
                  pl.BlockSpec((bb, HEAD_DIM), lambda h, i: (i, 0)),
                  pl.BlockSpec((bb, HEAD_DIM), lambda h, i: (i, 0)),
                  pl.BlockSpec((1, HEAD_DIM), lambda h, i: (0, h)),
                  pl.BlockSpec((1, 1, HEAD_DIM), lambda h, i: (h, 0, 0)),
                  pl.BlockSpec((bb, None, HEAD_DIM, HEAD_DIM), lambda h, i: (i, h, 0, 0))],
        out_specs=[pl.BlockSpec((bb, HEAD_DIM), lambda h, i: (i, h)),
                   pl.BlockSpec((bb, None, HEAD_DIM, HEAD_DIM), lambda h, i: (i, h, 0, 0))],
        out_shape=[jax.ShapeDtypeStruct((Bs, RET_WIDTH), BF16),
                   jax.ShapeDtypeStruct(s0.shape, F32)],
        scratch_shapes=[pltpu.VMEM((bb, HEAD_DIM), F32)] * 4,
        compiler_params=_params("parallel", "parallel"),
        name="retention_sample",
    )(h_main, h_main, h_main, h_main, cos_t, sin_t, gn_w.reshape(1, -1), lg_tab, s0)


def _q_latent_kernel(q_ref, w_ref, o_ref):
    o_ref[...] = _dot_nt(q_ref[:, :NOPE_DIM], w_ref[...]).astype(o_ref.dtype)


def q_latent(Q, wuk_heads):
    Bs = Q.shape[0]
    return pl.pallas_call(
        _q_latent_kernel,
        grid=(MLA_HEADS,),
        in_specs=[pl.BlockSpec((Bs, QK_PAD), lambda h: (0, h)),
                  pl.BlockSpec((None, KV_LORA, NOPE_DIM), lambda h: (h, 0, 0))],
        out_specs=pl.BlockSpec((Bs, KV_LORA), lambda h: (0, h)),
        out_shape=jax.ShapeDtypeStruct((Bs, MLA_HEADS * KV_LORA), BF16),
        compiler_params=_params("parallel"),
        name="q_latent",
    )(Q, wuk_heads)


def _paged_attn_kernel(pt_ref, ql_ref, qr_ref, cn_ref, rn_ref, *rest, G, npg):
    c_refs = rest[:G]
    r_refs = rest[G:2 * G]
    o_ref, m_scr, l_scr, acc_scr = rest[2 * G:]
    pg = pl.program_id(1)

    @pl.when(pg == 0)
    def _():
        m_scr[...] = jnp.full(m_scr.shape, NEG_INF, F32)
        l_scr[...] = jnp.zeros(l_scr.shape, F32)
        acc_scr[...] = jnp.zeros(acc_scr.shape, F32)

    ql = ql_ref[...]
    qr = qr_ref[...]
    cps = [c_refs[g][...].astype(BF16) for g in range(G)]
    ss = [(_dot_nt(ql, cps[g]) + _dot_nt(qr, r_refs[g][...].astype(BF16))) * MLA_SCALE for g in range(G)]
    m = m_scr[...]
    m_new = m
    for s in ss:
        m_new = jnp.maximum(m_new, jnp.max(s, axis=-1, keepdims=True))
    alpha = jnp.exp(m - m_new)
    l = l_scr[...] * alpha
    acc = acc_scr[...] * alpha
    for g in range(G):
        p = jnp.exp(ss[g] - m_new)
        l = l + jnp.sum(p, axis=-1, keepdims=True)
        acc = acc + _dot(p.astype(BF16), cps[g])
    m_scr[...] = m_new
    l_scr[...] = l
    acc_scr[...] = acc

    @pl.when(pg == npg - 1)
    def _():
        cn = cn_ref[...]
        s = (jnp.sum(ql.astype(F32) * cn, axis=-1, keepdims=True)
             + jnp.sum(qr.astype(F32) * rn_ref[...], axis=-1, keepdims=True)) * MLA_SCALE
        m2 = jnp.maximum(m_new, s)
        a2 = jnp.exp(m_new - m2)
        p = jnp.exp(s - m2)
        l2 = l * a2 + p
        acc2 = acc * a2 + p * cn
        o_ref[...] = (acc2 / l2).astype(o_ref.dtype)


def paged_attention(page_table, q_lat, q_rope, c_new, kr_new, cache_c, cache_r, *, G=8):
    Bs, n_pages = page_table.shape
    G = min(G, n_pages)
    npg = n_pages // G
    assert n_pages % G == 0

    def page_spec(width, g):
        return pl.BlockSpec((None, PAGE_SIZE, width), lambda b, p, pt, g=g: (pt[b, p * G + g], 0, 0))

    per_b = lambda d1, d2: pl.BlockSpec((None, d1, d2), lambda b, p, pt: (b, 0, 0))
    grid_spec = pltpu.PrefetchScalarGridSpec(
        num_scalar_prefetch=1,
        grid=(Bs, npg),
        in_specs=[per_b(MLA_HEADS, KV_LORA), per_b(MLA_HEADS, ROPE_DIM), per_b(1, KV_LORA), per_b(1, ROPE_DIM)]
                 + [page_spec(KV_LORA, g) for g in range(G)] + [page_spec(ROPE_DIM, g) for g in range(G)],
        out_specs=per_b(MLA_HEADS, KV_LORA),
        scratch_shapes=[pltpu.VMEM((MLA_HEADS, 1), F32), pltpu.VMEM((MLA_HEADS, 1), F32),
                        pltpu.VMEM((MLA_HEADS, KV_LORA), F32)],
    )
    return pl.pallas_call(
        functools.partial(_paged_attn_kernel, G=G, npg=npg),
        grid_spec=grid_spec,
        out_shape=jax.ShapeDtypeStruct((Bs, MLA_HEADS, KV_LORA), BF16),
        compiler_params=_params("parallel", "arbitrary"),
        name="paged_attention",
    )(page_table, q_lat, q_rope, c_new.reshape(Bs, 1, KV_LORA), kr_new.reshape(Bs, 1, ROPE_DIM),
      *([cache_c] * G), *([cache_r] * G))


def _v_up_kernel(o_ref, w_ref, out_ref):
    out_ref[...] = _dot(o_ref[...], w_ref[...]).astype(out_ref.dtype)


def v_up(o_lat2, wuv_heads):
    Bs = o_lat2.shape[0]
    return pl.pallas_call(
        _v_up_kernel,
        grid=(MLA_HEADS,),
        in_specs=[pl.BlockSpec((Bs, KV_LORA), lambda h: (0, h)),
                  pl.BlockSpec((None, KV_LORA, HEAD_DIM), lambda h: (h, 0, 0))],
        out_specs=pl.BlockSpec((Bs, HEAD_DIM), lambda h: (0, h)),
        out_shape=jax.ShapeDtypeStruct((Bs, MLA_HEADS * HEAD_DIM), BF16),
        compiler_params=_params("parallel"),
        name="v_up",
    )(o_lat2, wuv_heads)


def _prep_weights(w_in, w_q_up, w_uk, w_uv, w_o, w_gate, w_up, w_down):
    c0 = Q_LORA + KV_LORA
    w_main = jnp.concatenate([w_in[:, :c0], w_in[:, c0 + ROPE_DIM:]], axis=1).astype(BF16)
    w_kr = jnp.pad(w_in[:, c0:c0 + ROPE_DIM], ((0, 0), (0, 128 - ROPE_DIM))).astype(BF16)
    wq = w_q_up.reshape(Q_LORA, MLA_HEADS, NOPE_DIM + ROPE_DIM)
    wq_pad = jnp.pad(wq, ((0, 0), (0, 0), (0, QK_PAD - NOPE_DIM - ROPE_DIM))).reshape(Q_LORA, MLA_HEADS * QK_PAD).astype(BF16)
    wuk2 = w_uk.reshape(KV_LORA, MLA_HEADS * NOPE_DIM).astype(BF16)
    wuv2 = w_uv.reshape(KV_LORA, MLA_HEADS * HEAD_DIM).astype(BF16)
    wuk_h = jnp.transpose(w_uk, (1, 0, 2)).astype(BF16)
    wuv_h = jnp.transpose(w_uv, (1, 0, 2)).astype(BF16)
    F = w_gate.shape[1]
    Fp = -(-F // FFN_PAD) * FFN_PAD
    wg = jnp.pad(w_gate, ((0, 0), (0, Fp - F))).astype(BF16)
    wu = jnp.pad(w_up, ((0, 0), (0, Fp - F))).astype(BF16)
    wd = jnp.pad(w_down, ((0, Fp - F), (0, 0))).astype(BF16)
    return w_main, w_kr, wq_pad, wuk2, wuv2, wuk_h, wuv_h, w_o.astype(BF16), wg, wu, wd


def _ffn_tail(x, mla_o, ret_o, w_o, norm_ffn, wg, wu, wd, norm_final, bm):
    h = out_proj(mla_o, ret_o, w_o, x, bm=bm, bn=512)
    hn = rmsnorm(h, norm_ffn, BF16)
    a = matmul(hn, wg, w2=wu, bm=bm, bn=512, out_dtype=BF16, name="ffn_gate_up")
    Fp = wd.shape[0]
    out = matmul(a, wd, res=h, bm=bm, bn=1024, bk=Fp // 4, name="ffn_down")
    return rmsnorm(out, norm_final, F32)


def kernel(x_prompt, x_sample, cache_kv_latent, cache_k_rope, state_retention, page_table,
           w_in, norm_attn, q_a_norm, w_q_up, kv_a_norm, w_uk, w_uv, ret_gn_w, w_o,
           norm_ffn, w_gate, w_up, w_down, norm_final):
    B, L, D = x_prompt.shape
    Bs, T, _ = x_sample.shape
    depth = w_in.shape[0]
    assert depth == 1 and T == 1
    past = page_table.shape[1] * PAGE_SIZE
    (w_main, w_kr, wq_pad, wuk2, wuv2, wuk_h, wuv_h, wo, wg, wu, wd) = _prep_weights(
        w_in[0], w_q_up[0], w_uk[0], w_uv[0], w_o[0], w_gate[0], w_up[0], w_down[0])
    lg = jnp.log1p(-jnp.exp2(-5.0 - jnp.arange(RET_HEADS, dtype=F32)))
    lg_tab = jnp.broadcast_to(lg[:, None, None], (RET_HEADS, 1, HEAD_DIM))

    pos_p = jnp.arange(L, dtype=jnp.int32)
    pos_s = jnp.full((Bs,), past, dtype=jnp.int32)
    cos64_p, sin64_p = _rope_tables(pos_p, ROPE_DIM // 2, 128)
    cos128_p, sin128_p = _rope_tables(pos_p, HEAD_DIM // 2, 128)
    cos64_s, sin64_s = _rope_tables(pos_s, ROPE_DIM // 2, 128)
    cos128_s, sin128_s = _rope_tables(pos_s, HEAD_DIM // 2, 128)

    xp = x_prompt.reshape(B * L, D)
    bm = 1024
    xn = rmsnorm(xp, norm_attn[0], BF16)
    h_main = matmul(xn, w_main, bm=bm, bn=512, name="in_proj")
    kr_raw = matmul(xn, w_kr, bm=bm, bn=128, name="in_proj_kr")
    Q, c_p, kr_p, K, V = mla_prep(h_main, kr_raw, cos64_p, sin64_p, q_a_norm[0], kv_a_norm[0], wq_pad, wuk2, wuv2)
    mla_o = mla_prompt_attention(Q, K, V, B, L)
    ret_o, st_p = retention_prompt(h_main, cos128_p, sin128_p, ret_gn_w[0], lg_tab, B, L)
    y_p = _ffn_tail(xp, mla_o, ret_o, wo, norm_ffn[0], wg, wu, wd, norm_final, bm)

    xs = x_sample.reshape(Bs * T, D)
    xn = rmsnorm(xs, norm_attn[0], BF16)
    h_main = matmul(xn, w_main, bm=bm, bn=512, name="in_proj")
    kr_raw = matmul(xn, w_kr, bm=bm, bn=128, name="in_proj_kr")
    Qs, c_s, kr_s = mla_prep(h_main, kr_raw, cos64_s, sin64_s, q_a_norm[0], kv_a_norm[0], wq_pad)
    q_lat = q_latent(Qs, wuk_h).reshape(Bs, MLA_HEADS, KV_LORA)
    q_rope = Qs.reshape(Bs, MLA_HEADS, QK_PAD)[:, :, NOPE_DIM:NOPE_DIM + ROPE_DIM]
    o_lat = paged_attention(page_table, q_lat, q_rope, c_s, kr_s, cache_kv_latent[0], cache_k_rope[0])
    mla_o = v_up(o_lat.reshape(Bs, MLA_HEADS * KV_LORA), wuv_h)
    ret_o, st_s = retention_sample(h_main, cos128_s, sin128_s, ret_gn_w[0], lg_tab, state_retention[0])
    y_s = _ffn_tail(xs, mla_o, ret_o, wo, norm_ffn[0], wg, wu, wd, norm_final, bm)

    return (y_p.reshape(B, L, D), y_s.reshape(Bs, T, D),
            c_p.reshape(1, B, L, KV_LORA), kr_p.reshape(1, B, L, ROPE_DIM), st_p[None],
            c_s.reshape(1, Bs, T, KV_LORA), kr_s.reshape(1, Bs, T, ROPE_DIM), st_s[None])
```

```python
import functools
import math

import numpy as np
import jax
import jax.numpy as jnp
from jax import lax
from jax.experimental import pallas as pl
from jax.experimental.pallas import tpu as pltpu

F32 = jnp.float32
BF16 = jnp.bfloat16

HEAD_DIM = 128
MLA_HEADS = 16
RET_HEADS = 16
Q_LORA = 1536
KV_LORA = 512
NOPE_DIM = 128
ROPE_DIM = 64
RET_WIDTH = RET_HEADS * HEAD_DIM
QK_PAD = 256
MAIN_COLS = Q_LORA + KV_LORA + 4 * RET_WIDTH
MLA_SCALE = (NOPE_DIM + ROPE_DIM) ** -0.5
ROPE_BASE = 10000.0
RMS_EPS = 1e-6
GN_EPS = 1e-6
NEG_INF = -1e30
PAGE_SIZE = 128

VMEM_LIMIT = 56 * 1024 * 1024


def _params(*sem):
    return pltpu.CompilerParams(dimension_semantics=sem, vmem_limit_bytes=VMEM_LIMIT)


def _rms(x, g):
    ms = jnp.mean(x * x, axis=-1, keepdims=True)
    return x * lax.rsqrt(ms + RMS_EPS) * g


def _silu(x):
    return x * (1.0 / (1.0 + jnp.exp(-x)))


def _dot(a, b):
    return jnp.dot(a, b, preferred_element_type=F32)


def _dot_nt(a, b):
    return lax.dot_general(a, b, (((1,), (1,)), ((), ())), preferred_element_type=F32)


def _dot_tn(a, b):
    return lax.dot_general(a, b, (((0,), (0,)), ((), ())), preferred_element_type=F32)


def _rmsnorm_kernel(x_ref, g_ref, o_ref):
    o_ref[...] = _rms(x_ref[...], g_ref[...]).astype(o_ref.dtype)


def rmsnorm(x, g, out_dtype, br=256):
    R, D = x.shape
    br = min(br, R)
    return pl.pallas_call(
        _rmsnorm_kernel,
        grid=(R // br,),
        in_specs=[pl.BlockSpec((br, D), lambda i: (i, 0)),
                  pl.BlockSpec((1, D), lambda i: (0, 0))],
        out_specs=pl.BlockSpec((br, D), lambda i: (i, 0)),
        out_shape=jax.ShapeDtypeStruct((R, D), out_dtype),
        compiler_params=_params("parallel"),
        name="rmsnorm",
    )(x, g.reshape(1, D))


def _mm_kernel(*refs, nk, mode):
    if mode == "glu":
        x_ref, w_ref, w2_ref, o_ref = refs
        x = x_ref[...]
        g = _dot(x, w_ref[...])
        u = _dot(x, w2_ref[...])
        o_ref[...] = (_silu(g) * u).astype(o_ref.dtype)
        return
    if mode == "res":
        x_ref, w_ref, r_ref, o_ref = refs[:4]
    else:
        x_ref, w_ref, o_ref = refs[:3]
        r_ref = None

    def finish(acc):
        if r_ref is not None:
            acc = r_ref[...] + acc
        o_ref[...] = acc.astype(o_ref.dtype)

    if nk == 1:
        finish(_dot(x_ref[...], w_ref[...]))
        return
    acc_ref = refs[-1]
    k = pl.program_id(2)

    @pl.when(k == 0)
    def _():
        acc_ref[...] = _dot(x_ref[...], w_ref[...])

    @pl.when(k > 0)
    def _():
        acc_ref[...] += _dot(x_ref[...], w_ref[...])

    @pl.when(k == nk - 1)
    def _():
        finish(acc_ref[...])


def matmul(x, w, *, bm, bn, bk=None, out_dtype=F32, res=None, w2=None, name="matmul"):
    M, K = x.shape
    N = w.shape[1]
    bm = min(bm, M)
    bn = min(bn, N)
    bk = K if bk is None else bk
    nk = K // bk
    mode = "glu" if w2 is not None else ("res" if res is not None else "plain")
    assert M % bm == 0 and K % bk == 0 and bn % 128 == 0
    assert not (mode == "glu" and nk != 1)
    in_specs = [pl.BlockSpec((bm, bk), lambda i, j, k: (i, k)),
                pl.BlockSpec((bk, bn), lambda i, j, k: (k, j))]
    args = [x, w]
    if mode == "glu":
        in_specs.append(pl.BlockSpec((bk, bn), lambda i, j, k: (k, j)))
        args.append(w2)
    if mode == "res":
        in_specs.append(pl.BlockSpec((bm, bn), lambda i, j, k: (i, j)))
        args.append(res)
    scratch = [pltpu.VMEM((bm, bn), F32)] if nk > 1 else []
    return pl.pallas_call(
        functools.partial(_mm_kernel, nk=nk, mode=mode),
        grid=(M // bm, pl.cdiv(N, bn), nk),
        in_specs=in_specs,
        out_specs=pl.BlockSpec((bm, bn), lambda i, j, k: (i, j)),
        out_shape=jax.ShapeDtypeStruct((M, N), out_dtype),
        scratch_shapes=scratch,
        compiler_params=_params("parallel", "parallel", "arbitrary"),
        name=name,
    )(*args)


def _mm2_kernel(xa_ref, xb_ref, wa_ref, wb_ref, r_ref, o_ref):
    acc = _dot(xa_ref[...], wa_ref[...]) + _dot(xb_ref[...], wb_ref[...])
    o_ref[...] = r_ref[...] + acc


def out_proj(xa, xb, w, res, *, bm, bn):
    M, Ka = xa.shape
    Kb = xb.shape[1]
    N = w.shape[1]
    bm = min(bm, M)
    assert Ka == Kb and M % bm == 0 and N % bn == 0
    return pl.pallas_call(
        _mm2_kernel,
        grid=(M // bm, N // bn),
        in_specs=[pl.BlockSpec((bm, Ka), lambda i, j: (i, 0)),
                  pl.BlockSpec((bm, Kb), lambda i, j: (i, 0)),
                  pl.BlockSpec((Ka, bn), lambda i, j: (0, j)),
                  pl.BlockSpec((Kb, bn), lambda i, j: (1, j)),
                  pl.BlockSpec((bm, bn), lambda i, j: (i, j))],
        out_specs=pl.BlockSpec((bm, bn), lambda i, j: (i, j)),
        out_shape=jax.ShapeDtypeStruct((M, N), F32),
        compiler_params=_params("parallel", "parallel"),
        name="out_proj",
    )(xa, xb, w, w, res)


def _rope_tables(pos, half, width):
    inv = jnp.exp(-math.log(ROPE_BASE) * jnp.arange(half, dtype=F32) / half)
    ang = pos.astype(F32)[:, None] * inv[None, :]
    cos, sin = jnp.cos(ang), jnp.sin(ang)
    n = pos.shape[0]
    pad = width - 2 * half
    cos_t = jnp.concatenate([cos, cos, jnp.ones((n, pad), F32)], axis=-1)
    sin_t = jnp.concatenate([-sin, sin, jnp.zeros((n, pad), F32)], axis=-1)
    return cos_t, sin_t


def _rope64_tile(t, cos, sin):
    lane = lax.broadcasted_iota(jnp.int32, t.shape, 1)
    other = jnp.where(lane < ROPE_DIM // 2, pltpu.roll(t, 128 - ROPE_DIM // 2, axis=1),
                      pltpu.roll(t, ROPE_DIM // 2, axis=1))
    return t * cos + other * sin


def _rope128_tile(t, cos, sin):
    return t * cos + pltpu.roll(t, HEAD_DIM // 2, axis=1) * sin


def _mla_prep_kernel(h_ref, kr_ref, cos_ref, sin_ref, gq_ref, gkv_ref, wq_ref, *rest, want_kv):
    if want_kv:
        wuk_ref, wuv_ref, q_ref, c_ref, kro_ref, k_ref, v_ref = rest
    else:
        q_ref, c_ref, kro_ref = rest
    cos, sin = cos_ref[...], sin_ref[...]
    qn = _rms(h_ref[:, :Q_LORA], gq_ref[...]).astype(BF16)
    q = _dot(qn, wq_ref[...])
    for hh in range(MLA_HEADS):
        o = hh * QK_PAD
        q_ref[:, o:o + NOPE_DIM] = q[:, o:o + NOPE_DIM].astype(BF16)
        q_ref[:, o + NOPE_DIM:o + QK_PAD] = _rope64_tile(q[:, o + NOPE_DIM:o + QK_PAD], cos, sin).astype(BF16)
    c = _rms(h_ref[:, Q_LORA:Q_LORA + KV_LORA], gkv_ref[...])
    c_ref[...] = c
    kr = _rope64_tile(kr_ref[...], cos, sin)
    kro_ref[...] = kr[:, :ROPE_DIM]
    if want_kv:
        cb = c.astype(BF16)
        krb = kr.astype(BF16)
        kn = _dot(cb, wuk_ref[...])
        for hh in range(MLA_HEADS):
            o = hh * QK_PAD
            k_ref[:, o:o + NOPE_DIM] = kn[:, hh * NOPE_DIM:(hh + 1) * NOPE_DIM].astype(BF16)
            k_ref[:, o + NOPE_DIM:o + QK_PAD] = krb
        v_ref[...] = _dot(cb, wuv_ref[...]).astype(BF16)


def mla_prep(h_main, kr_raw, cos_t, sin_t, gq, gkv, wq_pad, wuk2=None, wuv2=None, *, bm=256):
    R = h_main.shape[0]
    bm = min(bm, R)
    ntab = cos_t.shape[0] // bm
    want_kv = wuk2 is not None
    row = lambda i: (i, 0)
    const = lambda i: (0, 0)
    in_specs = [pl.BlockSpec((bm, Q_LORA + KV_LORA), row),
                pl.BlockSpec((bm, 128), row),
                pl.BlockSpec((bm, 128), lambda i: (i % ntab, 0)),
                pl.BlockSpec((bm, 128), lambda i: (i % ntab, 0)),
                pl.BlockSpec((1, Q_LORA), const),
                pl.BlockSpec((1, KV_LORA), const),
                pl.BlockSpec(wq_pad.shape, const, pipeline_mode=pl.Buffered(1))]
    args = [h_main, kr_raw, cos_t, sin_t, gq.reshape(1, -1), gkv.reshape(1, -1), wq_pad]
    out_specs = [pl.BlockSpec((bm, MLA_HEADS * QK_PAD), row),
                 pl.BlockSpec((bm, KV_LORA), row),
                 pl.BlockSpec((bm, ROPE_DIM), row)]
    out_shape = [jax.ShapeDtypeStruct((R, MLA_HEADS * QK_PAD), BF16),
                 jax.ShapeDtypeStruct((R, KV_LORA), F32),
                 jax.ShapeDtypeStruct((R, ROPE_DIM), F32)]
    if want_kv:
        in_specs += [pl.BlockSpec(wuk2.shape, const), pl.BlockSpec(wuv2.shape, const)]
        args += [wuk2, wuv2]
        out_specs += [pl.BlockSpec((bm, MLA_HEADS * QK_PAD), row),
                      pl.BlockSpec((bm, MLA_HEADS * HEAD_DIM), row)]
        out_shape += [jax.ShapeDtypeStruct((R, MLA_HEADS * QK_PAD), BF16),
                      jax.ShapeDtypeStruct((R, MLA_HEADS * HEAD_DIM), BF16)]
    return pl.pallas_call(
        functools.partial(_mla_prep_kernel, want_kv=want_kv),
        grid=(R // bm,),
        in_specs=in_specs, out_specs=out_specs, out_shape=out_shape,
        compiler_params=_params("parallel"),
        name="mla_prep",
    )(*args)


def _causal_attn_kernel(q_ref, k_ref, v_ref, o_ref, *, L, bq):
    tri = (lax.broadcasted_iota(jnp.int32, (bq, bq), 0) >= lax.broadcasted_iota(jnp.int32, (bq, bq), 1))
    for qi in range(L // bq):
        lo, hi = qi * bq, (qi + 1) * bq
        q = q_ref[lo:hi, :]
        s_d = jnp.where(tri, _dot_nt(q, k_ref[lo:hi, :]) * MLA_SCALE, NEG_INF)
        m = jnp.max(s_d, axis=-1, keepdims=True)
        if qi > 0:
            s_p = _dot_nt(q, k_ref[:lo, :]) * MLA_SCALE
            m = jnp.maximum(m, jnp.max(s_p, axis=-1, keepdims=True))
        p_d = jnp.exp(s_d - m)
        l = jnp.sum(p_d, axis=-1, keepdims=True)
        acc = _dot(p_d.astype(BF16), v_ref[lo:hi, :])
        if qi > 0:
            p_p = jnp.exp(s_p - m)
            l = l + jnp.sum(p_p, axis=-1, keepdims=True)
            acc = acc + _dot(p_p.astype(BF16), v_ref[:lo, :])
        o_ref[lo:hi, :] = (acc / l).astype(o_ref.dtype)


def mla_prompt_attention(Q, K, V, B, L, *, bq=512):
    bq = min(bq, L)
    return pl.pallas_call(
        functools.partial(_causal_attn_kernel, L=L, bq=bq),
        grid=(B, MLA_HEADS),
        in_specs=[pl.BlockSpec((L, QK_PAD), lambda b, h: (b, h)),
                  pl.BlockSpec((L, QK_PAD), lambda b, h: (b, h)),
                  pl.BlockSpec((L, HEAD_DIM), lambda b, h: (b, h))],
        out_specs=pl.BlockSpec((L, HEAD_DIM), lambda b, h: (b, h)),
        out_shape=jax.ShapeDtypeStruct((B * L, MLA_HEADS * HEAD_DIM), BF16),
        compiler_params=_params("parallel", "parallel"),
        name="mla_prompt_attention",
    )(Q, K, V)


def _group_norm_gate(o, rg, gw):
    mu = jnp.mean(o, axis=-1, keepdims=True)
    d = o - mu
    var = jnp.mean(d * d, axis=-1, keepdims=True)
    return _silu(rg) * (d * lax.rsqrt(var + GN_EPS) * gw)


def _ret_prompt_kernel(rq_ref, rk_ref, rv_ref, rg_ref, cos_ref, sin_ref, gw_ref, lg_ref,
                       o_ref, s_ref, *, L, C):
    lg = lg_ref[0, :, :1]
    ii = lax.broadcasted_iota(jnp.int32, (C, C), 0)
    jj = lax.broadcasted_iota(jnp.int32, (C, C), 1)
    diff = (ii - jj).astype(F32)
    dec = jnp.where(diff >= 0, jnp.exp(jnp.maximum(diff, 0.0) * lg), 0.0)
    idx = lax.broadcasted_iota(jnp.int32, (C, 1), 0).astype(F32)
    q_dec = jnp.exp((idx + 1.0) * lg)
    k_dec = jnp.exp((C - 1.0 - idx) * lg)
    s_dec = jnp.exp(C * lg)
    gw = gw_ref[...]
    S = jnp.zeros((HEAD_DIM, HEAD_DIM), F32)
    for n in range(L // C):
        r = slice(n * C, (n + 1) * C)
        cos, sin = cos_ref[r, :], sin_ref[r, :]
        qc = _rope128_tile(rq_ref[r, :], cos, sin)
        kc = _rope128_tile(rk_ref[r, :], cos, sin) * (HEAD_DIM ** -0.5)
        qb = qc.astype(BF16)
        vb = rv_ref[r, :].astype(BF16)
        sc = _dot_nt(qb, kc.astype(BF16)) * dec
        o = _dot(sc.astype(BF16), vb) + _dot(qb, S.astype(BF16)) * q_dec
        S = S * s_dec + _dot_tn((kc * k_dec).astype(BF16), vb)
        o_ref[r, :] = _group_norm_gate(o, rg_ref[r, :], gw).astype(o_ref.dtype)
    s_ref[...] = S


def retention_prompt(h_main, cos_t, sin_t, gn_w, lg_tab, B, L, *, C=256):
    C = min(C, L)
    nb = RET_WIDTH // HEAD_DIM
    base = (Q_LORA + KV_LORA) // HEAD_DIM
    blk = lambda off: pl.BlockSpec((L, HEAD_DIM), lambda b, h, off=off: (b, off + h))
    return pl.pallas_call(
        functools.partial(_ret_prompt_kernel, L=L, C=C),
        grid=(B, RET_HEADS),
        in_specs=[blk(base), blk(base + nb), blk(base + 2 * nb), blk(base + 3 * nb),
                  pl.BlockSpec((L, HEAD_DIM), lambda b, h: (0, 0)),
                  pl.BlockSpec((L, HEAD_DIM), lambda b, h: (0, 0)),
                  pl.BlockSpec((1, HEAD_DIM), lambda b, h: (0, h)),
                  pl.BlockSpec((1, 1, HEAD_DIM), lambda b, h: (h, 0, 0))],
        out_specs=[pl.BlockSpec((L, HEAD_DIM), lambda b, h: (b, h)),
                   pl.BlockSpec((None, None, HEAD_DIM, HEAD_DIM), lambda b, h: (b, h, 0, 0))],
        out_shape=[jax.ShapeDtypeStruct((B * L, RET_WIDTH), BF16),
                   jax.ShapeDtypeStruct((B, RET_HEADS, HEAD_DIM, HEAD_DIM), F32)],
        compiler_params=_params("parallel", "parallel"),
        name="retention_prompt",
    )(h_main, h_main, h_main, h_main, cos_t, sin_t, gn_w.reshape(1, -1), lg_tab)


def _ret_sample_kernel(rq_ref, rk_ref, rv_ref, rg_ref, cos_ref, sin_ref, gw_ref, lg_ref, s0_ref,
                       o_ref, s_ref, q_scr, k_scr, v_scr, oc_scr, *, bb):
    gamma = jnp.exp(lg_ref[0, :, :1])
    cos, sin = cos_ref[...], sin_ref[...]
    q = _rope128_tile(rq_ref[...], cos, sin)
    k = _rope128_tile(rk_ref[...], cos, sin) * (HEAD_DIM ** -0.5)
    v = rv_ref[...]
    q_scr[...] = q
    k_scr[...] = k
    v_scr[...] = v
    eye = (lax.broadcasted_iota(jnp.int32, (HEAD_DIM, HEAD_DIM), 0)
           == lax.broadcasted_iota(jnp.int32, (HEAD_DIM, HEAD_DIM), 1))

    def body(b, carry):
        s0 = s0_ref[b]
        qb = jnp.broadcast_to(q_scr[pl.ds(b, 1), :], (8, HEAD_DIM)).astype(BF16)
        oc_scr[pl.ds(b, 1), :] = _dot(qb, s0.astype(BF16))[:1, :]
        kd = jnp.where(eye, jnp.broadcast_to(k_scr[pl.ds(b, 1), :], (HEAD_DIM, HEAD_DIM)), 0.0)
        vr = jnp.broadcast_to(v_scr[pl.ds(b, 1), :], (HEAD_DIM, HEAD_DIM))
        s_ref[b] = s0 * gamma + _dot(kd.astype(BF16), vr.astype(BF16))
        return carry

    lax.fori_loop(0, bb, body, 0, unroll=min(8, bb))
    qk = jnp.sum(q * k, axis=-1, keepdims=True)
    o = qk * v + oc_scr[...] * gamma
    o_ref[...] = _group_norm_gate(o, rg_ref[...], gw_ref[...]).astype(o_ref.dtype)


def retention_sample(h_main, cos_t, sin_t, gn_w, lg_tab, s0, *, bb=32):
    Bs = h_main.shape[0]
    bb = min(bb, Bs)
    nb = RET_WIDTH // HEAD_DIM
    base = (Q_LORA + KV_LORA) // HEAD_DIM
    blk = lambda off: pl.BlockSpec((bb, HEAD_DIM), lambda h, i, off=off: (i, off + h))
    return pl.pallas_call(
        functools.partial(_ret_sample_kernel, bb=bb),
        grid=(RET_HEADS, Bs // bb),
        in_specs=[blk(base), blk(base + nb), blk(base + 2 * nb), blk(base + 3 * nb),
                  pl.BlockSpec((bb, HEAD_DIM), lambda h, i: (i, 0)),
                  pl.BlockSpec((bb, HEAD_DIM), lambda h, i: (i, 0)),
                  pl.BlockSpec((1, HEAD_DIM), lambda h, i: (0, h)),
                  pl.BlockSpec((1, 1, HEAD_DIM), lambda h, i: (h, 0, 0)),
                  pl.BlockSpec((bb, None, HEAD_DIM, HEAD_DIM), lambda h, i: (i, h, 0, 0))],
        out_specs=[pl.BlockSpec((bb, HEAD_DIM), lambda h, i: (i, h)),
                   pl.BlockSpec((bb, None, HEAD_DIM, HEAD_DIM), lambda h, i: (i, h, 0, 0))],
        out_shape=[jax.ShapeDtypeStruct((Bs, RET_WIDTH), BF16),
                   jax.ShapeDtypeStruct(s0.shape, F32)],
        scratch_shapes=[pltpu.VMEM((bb, HEAD_DIM), F32)] * 4,
        compiler_params=_params("parallel", "parallel"),
        name="retention_sample",
    )(h_main, h_main, h_main, h_main, cos_t, sin_t, gn_w.reshape(1, -1), lg_tab, s0)


def _q_latent_kernel(q_ref, w_ref, o_ref):
    o_ref[...] = _dot_nt(q_ref[:, :NOPE_DIM], w_ref[...]).astype(o_ref.dtype)


def q_latent(Q, wuk_heads):
    Bs = Q.shape[0]
    return pl.pallas_call(
        _q_latent_kernel,
        grid=(MLA_HEADS,),
        in_specs=[pl.BlockSpec((Bs, QK_PAD), lambda h: (0, h)),
                  pl.BlockSpec((None, KV_LORA, NOPE_DIM), lambda h: (h, 0, 0))],
        out_specs=pl.BlockSpec((Bs, KV_LORA), lambda h: (0, h)),
        out_shape=jax.ShapeDtypeStruct((Bs, MLA_HEADS * KV_LORA), BF16),
        compiler_params=_params("parallel"),
        name="q_latent",
    )(Q, wuk_heads)


PAGED_SLOTS = 4


def _paged_attn_kernel(pt_ref, ql_ref, qr_ref, cn_ref, rn_ref, cache_c, cache_r, o_ref,
                       cbuf, rbuf, sem, *, G, NC, Bs):
    b = pl.program_id(0)
    D = PAGED_SLOTS - 1

    def chunk_copies(page_ids, slot):
        cps = []
        for g, pid in enumerate(page_ids):
            rows = pl.ds(g * PAGE_SIZE, PAGE_SIZE)
            cps.append(pltpu.make_async_copy(cache_c.at[pid], cbuf.at[slot, rows, :], sem.at[0, slot]))
            cps.append(pltpu.make_async_copy(cache_r.at[pid], rbuf.at[slot, :, rows], sem.at[1, slot]))
        return cps

    def start_chunk(bb, c):
        ids = [pt_ref[bb, c * G + g] for g in range(G)]
        for cp in chunk_copies(ids, c % PAGED_SLOTS):
            cp.start()

    def wait_chunk(c):
        for cp in chunk_copies([0] * G, c % PAGED_SLOTS):
            cp.wait()

    @pl.when(b == 0)
    def _():
        for c in range(D):
            start_chunk(0, c)

    ql = ql_ref[...]
    qr = qr_ref[...]
    m = jnp.full((MLA_HEADS, 1), NEG_INF, F32)
    l = jnp.zeros((MLA_HEADS, 1), F32)
    acc = jnp.zeros((MLA_HEADS, KV_LORA), F32)
    for c in range(NC):
        nxt = c + D
        if nxt < NC:
            start_chunk(b, nxt)
        else:
            @pl.when(b + 1 < Bs)
            def _(nxt=nxt):
                start_chunk(b + 1, nxt - NC)
        wait_chunk(c)
        slot = c % PAGED_SLOTS
        cp = cbuf[slot].astype(BF16)
        s = (_dot_nt(ql, cp) + _dot(qr, rbuf[slot].astype(BF16))) * MLA_SCALE
        m_new = jnp.maximum(m, jnp.max(s, axis=-1, keepdims=True))
        alpha = jnp.exp(m - m_new)
        p = jnp.exp(s - m_new)
        l = l * alpha + jnp.sum(p, axis=-1, keepdims=True)
        acc = acc * alpha + _dot(p.astype(BF16), cp)
        m = m_new

    cn = cn_ref[...]
    s = (jnp.sum(ql.astype(F32) * cn, axis=-1, keepdims=True)
         + jnp.sum(qr.astype(F32) * rn_ref[...], axis=-1, keepdims=True)) * MLA_SCALE
    m2 = jnp.maximum(m, s)
    a2 = jnp.exp(m - m2)
    p = jnp.exp(s - m2)
    o_ref[...] = ((acc * a2 + p * cn) / (l * a2 + p)).astype(o_ref.dtype)


def paged_attention(page_table, q_lat, q_rope, c_new, kr_new, cache_c, cache_rt, *, G=16):
    Bs, n_pages = page_table.shape
    G = min(G, n_pages // PAGED_SLOTS)
    NC = n_pages // G
    assert n_pages % G == 0 and NC % PAGED_SLOTS == 0
    per_b = lambda d1, d2: pl.BlockSpec((None, d1, d2), lambda b, pt: (b, 0, 0))
    grid_spec = pltpu.PrefetchScalarGridSpec(
        num_scalar_prefetch=1,
        grid=(Bs,),
        in_specs=[per_b(MLA_HEADS, KV_LORA), per_b(MLA_HEADS, ROPE_DIM), per_b(1, KV_LORA), per_b(1, ROPE_DIM),
                  pl.BlockSpec(memory_space=pl.ANY), pl.BlockSpec(memory_space=pl.ANY)],
        out_specs=per_b(MLA_HEADS, KV_LORA),
        scratch_shapes=[pltpu.VMEM((PAGED_SLOTS, G * PAGE_SIZE, KV_LORA), F32),
                        pltpu.VMEM((PAGED_SLOTS, ROPE_DIM, G * PAGE_SIZE), F32),
                        pltpu.SemaphoreType.DMA((2, PAGED_SLOTS))],
    )
    return pl.pallas_call(
        functools.partial(_paged_attn_kernel, G=G, NC=NC, Bs=Bs),
        grid_spec=grid_spec,
        out_shape=jax.ShapeDtypeStruct((Bs, MLA_HEADS, KV_LORA), BF16),
        compiler_params=_params("arbitrary"),
        name="paged_attention",
    )(page_table, q_lat, q_rope, c_new.reshape(Bs, 1, KV_LORA), kr_new.reshape(Bs, 1, ROPE_DIM),
      cache_c, cache_rt)


def _v_up_kernel(o_ref, w_ref, out_ref):
    out_ref[...] = _dot(o_ref[...], w_ref[...]).astype(out_ref.dtype)


def v_up(o_lat2, wuv_heads):
    Bs = o_lat2.shape[0]
    return pl.pallas_call(
        _v_up_kernel,
        grid=(MLA_HEADS,),
        in_specs=[pl.BlockSpec((Bs, KV_LORA), lambda h: (0, h)),
                  pl.BlockSpec((None, KV_LORA, HEAD_DIM), lambda h: (h, 0, 0))],
        out_specs=pl.BlockSpec((Bs, HEAD_DIM), lambda h: (0, h)),
        out_shape=jax.ShapeDtypeStruct((Bs, MLA_HEADS * HEAD_DIM), BF16),
        compiler_params=_params("parallel"),
        name="v_up",
    )(o_lat2, wuv_heads)


def _prep_weights(w_in, w_q_up, w_uk, w_uv, w_o, w_gate, w_up, w_down):
    c0 = Q_LORA + KV_LORA
    w_main = jnp.concatenate([w_in[:, :c0], w_in[:, c0 + ROPE_DIM:]], axis=1).astype(BF16)
    w_kr = jnp.pad(w_in[:, c0:c0 + ROPE_DIM], ((0, 0), (0, 128 - ROPE_DIM))).astype(BF16)
    wq = w_q_up.reshape(Q_LORA, MLA_HEADS, NOPE_DIM + ROPE_DIM)
    wq_pad = jnp.pad(wq, ((0, 0), (0, 0), (0, QK_PAD - NOPE_DIM - ROPE_DIM))).reshape(Q_LORA, MLA_HEADS * QK_PAD).astype(BF16)
    wuk2 = w_uk.reshape(KV_LORA, MLA_HEADS * NOPE_DIM).astype(BF16)
    wuv2 = w_uv.reshape(KV_LORA, MLA_HEADS * HEAD_DIM).astype(BF16)
    wuk_h = jnp.transpose(w_uk, (1, 0, 2)).astype(BF16)
    wuv_h = jnp.transpose(w_uv, (1, 0, 2)).astype(BF16)
    return (w_main, w_kr, wq_pad, wuk2, wuv2, wuk_h, wuv_h, w_o.astype(BF16),
            w_gate.astype(BF16), w_up.astype(BF16), w_down.astype(BF16))


def _ffn_tail(x, mla_o, ret_o, w_o, norm_ffn, wg, wu, wd, norm_final, bm):
    h = out_proj(mla_o, ret_o, w_o, x, bm=bm, bn=512)
    hn = rmsnorm(h, norm_ffn, BF16)
    a = matmul(hn, wg, w2=wu, bm=bm, bn=512, out_dtype=BF16, name="ffn_gate_up")
    out = matmul(a, wd, res=h, bm=bm // 2, bn=512, name="ffn_down")
    return rmsnorm(out, norm_final, F32)


def kernel(x_prompt, x_sample, cache_kv_latent, cache_k_rope, state_retention, page_table,
           w_in, norm_attn, q_a_norm, w_q_up, kv_a_norm, w_uk, w_uv, ret_gn_w, w_o,
           norm_ffn, w_gate, w_up, w_down, norm_final):
    B, L, D = x_prompt.shape
    Bs, T, _ = x_sample.shape
    depth = w_in.shape[0]
    assert depth == 1 and T == 1
    past = page_table.shape[1] * PAGE_SIZE
    (w_main, w_kr, wq_pad, wuk2, wuv2, wuk_h, wuv_h, wo, wg, wu, wd) = _prep_weights(
        w_in[0], w_q_up[0], w_uk[0], w_uv[0], w_o[0], w_gate[0], w_up[0], w_down[0])
    lg = jnp.log1p(-jnp.exp2(-5.0 - jnp.arange(RET_HEADS, dtype=F32)))
    lg_tab = jnp.broadcast_to(lg[:, None, None], (RET_HEADS, 1, HEAD_DIM))

    pos_p = jnp.arange(L, dtype=jnp.int32)
    pos_s = jnp.full((Bs,), past, dtype=jnp.int32)
    cos64_p, sin64_p = _rope_tables(pos_p, ROPE_DIM // 2, 128)
    cos128_p, sin128_p = _rope_tables(pos_p, HEAD_DIM // 2, 128)
    cos64_s, sin64_s = _rope_tables(pos_s, ROPE_DIM // 2, 128)
    cos128_s, sin128_s = _rope_tables(pos_s, HEAD_DIM // 2, 128)

    xp = x_prompt.reshape(B * L, D)
    bm = 1024
    xn = rmsnorm(xp, norm_attn[0], BF16)
    h_main = matmul(xn, w_main, bm=bm, bn=512, name="in_proj")
    kr_raw = matmul(xn, w_kr, bm=bm, bn=128, name="in_proj_kr")
    Q, c_p, kr_p, K, V = mla_prep(h_main, kr_raw, cos64_p, sin64_p, q_a_norm[0], kv_a_norm[0], wq_pad, wuk2, wuv2)
    mla_o = mla_prompt_attention(Q, K, V, B, L)
    ret_o, st_p = retention_prompt(h_main, cos128_p, sin128_p, ret_gn_w[0], lg_tab, B, L)
    y_p = _ffn_tail(xp, mla_o, ret_o, wo, norm_ffn[0], wg, wu, wd, norm_final, bm)

    xs = x_sample.reshape(Bs * T, D)
    xn = rmsnorm(xs, norm_attn[0], BF16)
    h_main = matmul(xn, w_main, bm=bm, bn=512, name="in_proj")
    kr_raw = matmul(xn, w_kr, bm=bm, bn=128, name="in_proj_kr")
    Qs, c_s, kr_s = mla_prep(h_main, kr_raw, cos64_s, sin64_s, q_a_norm[0], kv_a_norm[0], wq_pad)
    q_lat = q_latent(Qs, wuk_h).reshape(Bs, MLA_HEADS, KV_LORA)
    q_rope = Qs.reshape(Bs, MLA_HEADS, QK_PAD)[:, :, NOPE_DIM:NOPE_DIM + ROPE_DIM]
    cache_rt = jnp.swapaxes(cache_k_rope[0], 1, 2)
    o_lat = paged_attention(page_table, q_lat, q_rope, c_s, kr_s, cache_kv_latent[0], cache_rt)
    mla_o = v_up(o_lat.reshape(Bs, MLA_HEADS * KV_LORA), wuv_h)
    ret_o, st_s = retention_sample(h_main, cos128_s, sin128_s, ret_gn_w[0], lg_tab, state_retention[0])
    y_s = _ffn_tail(xs, mla_o, ret_o, wo, norm_ffn[0], wg, wu, wd, norm_final, bm)

    return (y_p.reshape(B, L, D), y_s.reshape(Bs, T, D),
            c_p.reshape(1, B, L, KV_LORA), kr_p.reshape(1, B, L, ROPE_DIM), st_p[None],
            c_s.reshape(1, Bs, T, KV_LORA), kr_s.reshape(1, Bs, T, ROPE_DIM), st_s[None])
```

```python
import functools
import math

import jax
import jax.numpy as jnp
from jax import lax
from jax.experimental import pallas as pl
from jax.experimental.pallas import tpu as pltpu

F32 = jnp.float32
BF16 = jnp.bfloat16

HEAD_DIM = 128
MLA_HEADS = 16
RET_HEADS = 16
Q_LORA = 1536
KV_LORA = 512
NOPE_DIM = 128
ROPE_DIM = 64
RET_WIDTH = RET_HEADS * HEAD_DIM
QK_PAD = 256
MLA_SCALE = (NOPE_DIM + ROPE_DIM) ** -0.5
ROPE_BASE = 10000.0
RMS_EPS = 1e-6
GN_EPS = 1e-6
NEG_INF = -1e30
PAGE_SIZE = 128
BF16_ROWS = 16

VMEM_LIMIT = 56 * 1024 * 1024


def _params(*sem):
    return pltpu.CompilerParams(dimension_semantics=sem, vmem_limit_bytes=VMEM_LIMIT)


def _row_block(rows, target):
    best = None
    for d in range(BF16_ROWS, min(rows, target) + 1, BF16_ROWS):
        if rows % d == 0:
            best = d
    assert best is not None, rows
    return best


def _rms(x, g):
    ms = jnp.mean(x * x, axis=-1, keepdims=True)
    return x * lax.rsqrt(ms + RMS_EPS) * g


def _silu(x):
    return x * (1.0 / (1.0 + jnp.exp(-x)))


def _dot(a, b):
    return jnp.dot(a, b, preferred_element_type=F32)


def _dot_nt(a, b):
    return lax.dot_general(a, b, (((1,), (1,)), ((), ())), preferred_element_type=F32)


def _dot_tn(a, b):
    return lax.dot_general(a, b, (((0,), (0,)), ((), ())), preferred_element_type=F32)


def _rmsnorm_kernel(x_ref, g_ref, o_ref):
    o_ref[...] = _rms(x_ref[...], g_ref[...]).astype(o_ref.dtype)


def rmsnorm(x, g, out_dtype, br=256):
    R, D = x.shape
    br = min(br, R)
    return pl.pallas_call(
        _rmsnorm_kernel,
        grid=(R // br,),
        in_specs=[pl.BlockSpec((br, D), lambda i: (i, 0)),
                  pl.BlockSpec((1, D), lambda i: (0, 0))],
        out_specs=pl.BlockSpec((br, D), lambda i: (i, 0)),
        out_shape=jax.ShapeDtypeStruct((R, D), out_dtype),
        compiler_params=_params("parallel"),
        name="rmsnorm",
    )(x, g.reshape(1, D))


def _rmsnorm2_kernel(xp_ref, xs_ref, g_ref, o_ref, *, npb):
    i = pl.program_id(0)

    @pl.when(i < npb)
    def _():
        o_ref[...] = _rms(xp_ref[...], g_ref[...]).astype(o_ref.dtype)

    @pl.when(i >= npb)
    def _():
        o_ref[...] = _rms(xs_ref[...], g_ref[...]).astype(o_ref.dtype)


def rmsnorm_rows(xp, xs, g):
    Rp, D = xp.shape
    Rs = xs.shape[0]
    assert Rp % Rs == 0
    npb = Rp // Rs
    return pl.pallas_call(
        functools.partial(_rmsnorm2_kernel, npb=npb),
        grid=(npb + 1,),
        in_specs=[pl.BlockSpec((Rs, D), lambda i: (jnp.minimum(i, npb - 1), 0)),
                  pl.BlockSpec((Rs, D), lambda i: (0, 0)),
                  pl.BlockSpec((1, D), lambda i: (0, 0))],
        out_specs=pl.BlockSpec((Rs, D), lambda i: (i, 0)),
        out_shape=jax.ShapeDtypeStruct((Rp + Rs, D), BF16),
        compiler_params=_params("arbitrary"),
        name="rmsnorm_rows",
    )(xp, xs, g.reshape(1, D))


def _mm_kernel(*refs, mode):
    if mode == "glu":
        x_ref, w_ref, w2_ref, o_ref = refs
        x = x_ref[...]
        g = _dot(x, w_ref[...].astype(BF16))
        u = _dot(x, w2_ref[...].astype(BF16))
        o_ref[...] = (_silu(g) * u).astype(o_ref.dtype)
    elif mode == "res":
        x_ref, w_ref, r_ref, o_ref = refs
        o_ref[...] = (r_ref[...] + _dot(x_ref[...], w_ref[...].astype(BF16))).astype(o_ref.dtype)
    else:
        x_ref, w_ref, o_ref = refs
        o_ref[...] = _dot(x_ref[...], w_ref[...].astype(BF16)).astype(o_ref.dtype)


def matmul(x, w, *, bm, bn, out_dtype=F32, res=None, w2=None, m_rows=None, m_off=0, name="matmul"):
    K = x.shape[1]
    N = w.shape[1]
    M = x.shape[0] if m_rows is None else m_rows
    bn = min(bn, N)
    mode = "glu" if w2 is not None else ("res" if res is not None else "plain")
    assert M % bm == 0 and N % bn == 0 and bn % 128 == 0
    in_specs = [pl.BlockSpec((bm, K), lambda i, j: (i + m_off, 0)),
                pl.BlockSpec((K, bn), lambda i, j: (0, j))]
    args = [x, w]
    if mode == "glu":
        in_specs.append(pl.BlockSpec((K, bn), lambda i, j: (0, j)))
        args.append(w2)
    if mode == "res":
        in_specs.append(pl.BlockSpec((bm, bn), lambda i, j: (i, j)))
        args.append(res)
    return pl.pallas_call(
        functools.partial(_mm_kernel, mode=mode),
        grid=(M // bm, N // bn),
        in_specs=in_specs,
        out_specs=pl.BlockSpec((bm, bn), lambda i, j: (i, j)),
        out_shape=jax.ShapeDtypeStruct((M, N), out_dtype),
        compiler_params=_params("parallel", "parallel"),
        name=name,
    )(*args)


def _out_proj_kernel(xa_ref, xb_ref, wa_ref, wb_ref, r_ref, g_ref, hn_in_ref, h_ref, hn_ref, row_scr,
                     *, nj, bn):
    del hn_in_ref
    j = pl.program_id(1)
    h = r_ref[...] + (_dot(xa_ref[...], wa_ref[...]) + _dot(xb_ref[...], wb_ref[...]))
    h_ref[...] = h
    row_scr[j] = h

    @pl.when(j == nj - 1)
    def _():
        n_cols = nj * bn
        ss = None
        for jj in range(nj):
            hj = row_scr[jj]
            s = jnp.sum(hj * hj, axis=-1, keepdims=True)
            ss = s if ss is None else ss + s
        inv = lax.rsqrt(ss / n_cols + RMS_EPS)
        for jj in range(nj):
            c = slice(jj * bn, (jj + 1) * bn)
            hn_ref[:, c] = (row_scr[jj] * inv * g_ref[:, c]).astype(hn_ref.dtype)


def out_proj(xa, xb, w, res, g, hn_buf, *, bm, bn, hn_off=0):
    M, Ka = xa.shape
    N = w.shape[1]
    nj = N // bn
    assert xb.shape[1] == Ka and w.shape[0] == 2 * Ka and M % bm == 0 and N % bn == 0
    assert hn_buf.shape[1] == N and hn_buf.dtype == BF16
    return pl.pallas_call(
        functools.partial(_out_proj_kernel, nj=nj, bn=bn),
        grid=(M // bm, nj),
        in_specs=[pl.BlockSpec((bm, Ka), lambda i, j: (i, 0)),
                  pl.BlockSpec((bm, Ka), lambda i, j: (i, 0)),
                  pl.BlockSpec((Ka, bn), lambda i, j: (0, j)),
                  pl.BlockSpec((Ka, bn), lambda i, j: (1, j)),
                  pl.BlockSpec((bm, bn), lambda i, j: (i, j)),
                  pl.BlockSpec((1, N), lambda i, j: (0, 0)),
                  pl.BlockSpec(memory_space=pl.ANY)],
        out_specs=[pl.BlockSpec((bm, bn), lambda i, j: (i, j)),
                   pl.BlockSpec((bm, N), lambda i, j: (i + hn_off, 0))],
        out_shape=[jax.ShapeDtypeStruct((M, N), F32),
                   jax.ShapeDtypeStruct(hn_buf.shape, BF16)],
        scratch_shapes=[pltpu.VMEM((nj, bm, bn), F32)],
        input_output_aliases={6: 1},
        compiler_params=_params("parallel", "arbitrary"),
        name="out_proj",
    )(xa, xb, w, w, res, g.reshape(1, N), hn_buf)


def _rope_tables(pos, half, width):
    inv = jnp.exp(-math.log(ROPE_BASE) * jnp.arange(half, dtype=F32) / half)
    ang = pos.astype(F32)[:, None] * inv[None, :]
    cos, sin = jnp.cos(ang), jnp.sin(ang)
    n = pos.shape[0]
    pad = width - 2 * half
    cos_t = jnp.concatenate([cos, cos, jnp.ones((n, pad), F32)], axis=-1)
    sin_t = jnp.concatenate([-sin, sin, jnp.zeros((n, pad), F32)], axis=-1)
    return cos_t, sin_t


def _rope64_tile(t, cos, sin):
    lane = lax.broadcasted_iota(jnp.int32, t.shape, 1)
    other = jnp.where(lane < ROPE_DIM // 2, pltpu.roll(t, 128 - ROPE_DIM // 2, axis=1),
                      pltpu.roll(t, ROPE_DIM // 2, axis=1))
    return t * cos + other * sin


def _rope128_tile(t, cos, sin):
    return t * cos + pltpu.roll(t, HEAD_DIM // 2, axis=1) * sin


def _mla_prep_kernel(h_ref, kr_ref, cos_ref, sin_ref, gq_ref, gkv_ref, wq_ref, wuk_ref, wuv_ref,
                     q_ref, c_ref, kro_ref, k_ref, v_ref):
    cos, sin = cos_ref[...], sin_ref[...]
    qn = _rms(h_ref[:, :Q_LORA], gq_ref[...]).astype(BF16)
    q = _dot(qn, wq_ref[...])
    for hh in range(MLA_HEADS):
        o = hh * QK_PAD
        q_ref[:, o:o + NOPE_DIM] = q[:, o:o + NOPE_DIM].astype(BF16)
        q_ref[:, o + NOPE_DIM:o + QK_PAD] = _rope64_tile(q[:, o + NOPE_DIM:o + QK_PAD], cos, sin).astype(BF16)
    c = _rms(h_ref[:, Q_LORA:Q_LORA + KV_LORA], gkv_ref[...])
    c_ref[...] = c
    kr = _rope64_tile(kr_ref[...], cos, sin)
    kro_ref[...] = kr[:, :ROPE_DIM]
    cb = c.astype(BF16)
    krb = kr.astype(BF16)
    kn = _dot(cb, wuk_ref[...])
    for hh in range(MLA_HEADS):
        o = hh * QK_PAD
        k_ref[:, o:o + NOPE_DIM] = kn[:, hh * NOPE_DIM:(hh + 1) * NOPE_DIM].astype(BF16)
        k_ref[:, o + NOPE_DIM:o + QK_PAD] = krb
    v_ref[...] = _dot(cb, wuv_ref[...]).astype(BF16)


def mla_prep(h_main, kr_raw, cos_t, sin_t, gq, gkv, wq_pad, wuk2, wuv2, *, bm):
    R = h_main.shape[0]
    assert R % bm == 0
    row = lambda i: (i, 0)
    const = lambda i: (0, 0)
    return pl.pallas_call(
        _mla_prep_kernel,
        grid=(R // bm,),
        in_specs=[pl.BlockSpec((bm, Q_LORA + KV_LORA), row),
                  pl.BlockSpec((bm, 128), row),
                  pl.BlockSpec((bm, 128), row),
                  pl.BlockSpec((bm, 128), row),
                  pl.BlockSpec((1, Q_LORA), const),
                  pl.BlockSpec((1, KV_LORA), const),
                  pl.BlockSpec(wq_pad.shape, const, pipeline_mode=pl.Buffered(1)),
                  pl.BlockSpec(wuk2.shape, const),
                  pl.BlockSpec(wuv2.shape, const)],
        out_specs=[pl.BlockSpec((bm, MLA_HEADS * QK_PAD), row),
                   pl.BlockSpec((bm, KV_LORA), row),
                   pl.BlockSpec((bm, ROPE_DIM), row),
                   pl.BlockSpec((bm, MLA_HEADS * QK_PAD), row),
                   pl.BlockSpec((bm, MLA_HEADS * HEAD_DIM), row)],
        out_shape=[jax.ShapeDtypeStruct((R, MLA_HEADS * QK_PAD), BF16),
                   jax.ShapeDtypeStruct((R, KV_LORA), F32),
                   jax.ShapeDtypeStruct((R, ROPE_DIM), F32),
                   jax.ShapeDtypeStruct((R, MLA_HEADS * QK_PAD), BF16),
                   jax.ShapeDtypeStruct((R, MLA_HEADS * HEAD_DIM), BF16)],
        compiler_params=_params("parallel"),
        name="mla_prep",
    )(h_main, kr_raw, cos_t, sin_t, gq.reshape(1, -1), gkv.reshape(1, -1), wq_pad, wuk2, wuv2)


def _causal_attn_kernel(q_ref, k_ref, v_ref, o_ref, *, L, bq):
    tri = (lax.broadcasted_iota(jnp.int32, (bq, bq), 0) >= lax.broadcasted_iota(jnp.int32, (bq, bq), 1))
    for qi in range(L // bq):
        lo, hi = qi * bq, (qi + 1) * bq
        q = q_ref[lo:hi, :]
        s_d = jnp.where(tri, _dot_nt(q, k_ref[lo:hi, :]) * MLA_SCALE, NEG_INF)
        m = jnp.max(s_d, axis=-1, keepdims=True)
        if qi > 0:
            s_p = _dot_nt(q, k_ref[:lo, :]) * MLA_SCALE
            m = jnp.maximum(m, jnp.max(s_p, axis=-1, keepdims=True))
        p_d = jnp.exp(s_d - m)
        l = jnp.sum(p_d, axis=-1, keepdims=True)
        acc = _dot(p_d.astype(BF16), v_ref[lo:hi, :])
        if qi > 0:
            p_p = jnp.exp(s_p - m)
            l = l + jnp.sum(p_p, axis=-1, keepdims=True)
            acc = acc + _dot(p_p.astype(BF16), v_ref[:lo, :])
        o_ref[lo:hi, :] = (acc / l).astype(o_ref.dtype)


def mla_prompt_attention(Q, K, V, B, L, *, bq=512):
    bq = min(bq, L)
    return pl.pallas_call(
        functools.partial(_causal_attn_kernel, L=L, bq=bq),
        grid=(B, MLA_HEADS),
        in_specs=[pl.BlockSpec((L, QK_PAD), lambda b, h: (b, h)),
                  pl.BlockSpec((L, QK_PAD), lambda b, h: (b, h)),
                  pl.BlockSpec((L, HEAD_DIM), lambda b, h: (b, h))],
        out_specs=pl.BlockSpec((L, HEAD_DIM), lambda b, h: (b, h)),
        out_shape=jax.ShapeDtypeStruct((B * L, MLA_HEADS * HEAD_DIM), BF16),
        compiler_params=_params("parallel", "parallel"),
        name="mla_prompt_attention",
    )(Q, K, V)


def _group_norm_gate(o, rg, gw):
    mu = jnp.mean(o, axis=-1, keepdims=True)
    d = o - mu
    var = jnp.mean(d * d, axis=-1, keepdims=True)
    return _silu(rg) * (d * lax.rsqrt(var + GN_EPS) * gw)


def _ret_prompt_kernel(rq_ref, rk_ref, rv_ref, rg_ref, cos_ref, sin_ref, gw_ref, lg_ref,
                       o_ref, s_ref, *, L, C):
    lg = lg_ref[0, :, :1]
    ii = lax.broadcasted_iota(jnp.int32, (C, C), 0)
    jj = lax.broadcasted_iota(jnp.int32, (C, C), 1)
    diff = (ii - jj).astype(F32)
    dec = jnp.where(diff >= 0, jnp.exp(jnp.maximum(diff, 0.0) * lg), 0.0)
    idx = lax.broadcasted_iota(jnp.int32, (C, 1), 0).astype(F32)
    q_dec = jnp.exp((idx + 1.0) * lg)
    k_dec = jnp.exp((C - 1.0 - idx) * lg)
    s_dec = jnp.exp(C * lg)
    gw = gw_ref[...]
    S = jnp.zeros((HEAD_DIM, HEAD_DIM), F32)
    for n in range(L // C):
        r = slice(n * C, (n + 1) * C)
        cos, sin = cos_ref[r, :], sin_ref[r, :]
        qc = _rope128_tile(rq_ref[r, :], cos, sin)
        kc = _rope128_tile(rk_ref[r, :], cos, sin) * (HEAD_DIM ** -0.5)
        qb = qc.astype(BF16)
        vb = rv_ref[r, :].astype(BF16)
        sc = _dot_nt(qb, kc.astype(BF16)) * dec
        o = _dot(sc.astype(BF16), vb) + _dot(qb, S.astype(BF16)) * q_dec
        S = S * s_dec + _dot_tn((kc * k_dec).astype(BF16), vb)
        o_ref[r, :] = _group_norm_gate(o, rg_ref[r, :], gw).astype(o_ref.dtype)
    s_ref[...] = S


def retention_prompt(h_main, cos_t, sin_t, gn_w, lg_tab, B, L, *, C=256):
    C = min(C, L)
    nb = RET_WIDTH // HEAD_DIM
    base = (Q_LORA + KV_LORA) // HEAD_DIM
    blk = lambda off: pl.BlockSpec((L, HEAD_DIM), lambda b, h, off=off: (b, off + h))
    return pl.pallas_call(
        functools.partial(_ret_prompt_kernel, L=L, C=C),
        grid=(B, RET_HEADS),
        in_specs=[blk(base), blk(base + nb), blk(base + 2 * nb), blk(base + 3 * nb),
                  pl.BlockSpec((L, HEAD_DIM), lambda b, h: (0, 0)),
                  pl.BlockSpec((L, HEAD_DIM), lambda b, h: (0, 0)),
                  pl.BlockSpec((1, HEAD_DIM), lambda b, h: (0, h)),
                  pl.BlockSpec((1, 1, HEAD_DIM), lambda b, h: (h, 0, 0))],
        out_specs=[pl.BlockSpec((L, HEAD_DIM), lambda b, h: (b, h)),
                   pl.BlockSpec((None, None, HEAD_DIM, HEAD_DIM), lambda b, h: (b, h, 0, 0))],
        out_shape=[jax.ShapeDtypeStruct((B * L, RET_WIDTH), BF16),
                   jax.ShapeDtypeStruct((B, RET_HEADS, HEAD_DIM, HEAD_DIM), F32)],
        compiler_params=_params("parallel", "parallel"),
        name="retention_prompt",
    )(h_main, h_main, h_main, h_main, cos_t, sin_t, gn_w.reshape(1, -1), lg_tab)


def _ret_sample_kernel(rq_ref, rk_ref, rv_ref, rg_ref, cos_ref, sin_ref, gw_ref, lg_ref, s0_ref,
                       o_ref, s_ref, q_scr, k_scr, v_scr, oc_scr, *, bb):
    gamma = jnp.exp(lg_ref[0, :, :1])
    cos, sin = cos_ref[...], sin_ref[...]
    q = _rope128_tile(rq_ref[...], cos, sin)
    k = _rope128_tile(rk_ref[...], cos, sin) * (HEAD_DIM ** -0.5)
    v = rv_ref[...]
    q_scr[...] = q
    k_scr[...] = k
    v_scr[...] = v
    eye = (lax.broadcasted_iota(jnp.int32, (HEAD_DIM, HEAD_DIM), 0)
           == lax.broadcasted_iota(jnp.int32, (HEAD_DIM, HEAD_DIM), 1))

    def body(b, carry):
        s0 = s0_ref[b]
        qb = jnp.broadcast_to(q_scr[pl.ds(b, 1), :], (8, HEAD_DIM)).astype(BF16)
        oc_scr[pl.ds(b, 1), :] = _dot(qb, s0.astype(BF16))[:1, :]
        kd = jnp.where(eye, jnp.broadcast_to(k_scr[pl.ds(b, 1), :], (HEAD_DIM, HEAD_DIM)), 0.0)
        vr = jnp.broadcast_to(v_scr[pl.ds(b, 1), :], (HEAD_DIM, HEAD_DIM))
        s_ref[b] = s0 * gamma + _dot(kd.astype(BF16), vr.astype(BF16))
        return carry

    lax.fori_loop(0, bb, body, 0, unroll=min(8, bb))
    qk = jnp.sum(q * k, axis=-1, keepdims=True)
    o = qk * v + oc_scr[...] * gamma
    o_ref[...] = _group_norm_gate(o, rg_ref[...], gw_ref[...]).astype(o_ref.dtype)


def retention_sample(h_main, cos_t, sin_t, gn_w, lg_tab, s0, *, row0, bb=32):
    Bs = s0.shape[0]
    bb = min(bb, Bs)
    assert Bs % bb == 0 and row0 % bb == 0
    r0 = row0 // bb
    nb = RET_WIDTH // HEAD_DIM
    base = (Q_LORA + KV_LORA) // HEAD_DIM
    blk = lambda off: pl.BlockSpec((bb, HEAD_DIM), lambda h, i, off=off: (r0 + i, off + h))
    return pl.pallas_call(
        functools.partial(_ret_sample_kernel, bb=bb),
        grid=(RET_HEADS, Bs // bb),
        in_specs=[blk(base), blk(base + nb), blk(base + 2 * nb), blk(base + 3 * nb),
                  pl.BlockSpec((bb, HEAD_DIM), lambda h, i: (i, 0)),
                  pl.BlockSpec((bb, HEAD_DIM), lambda h, i: (i, 0)),
                  pl.BlockSpec((1, HEAD_DIM), lambda h, i: (0, h)),
                  pl.BlockSpec((1, 1, HEAD_DIM), lambda h, i: (h, 0, 0)),
                  pl.BlockSpec((bb, None, HEAD_DIM, HEAD_DIM), lambda h, i: (i, h, 0, 0))],
        out_specs=[pl.BlockSpec((bb, HEAD_DIM), lambda h, i: (i, h)),
                   pl.BlockSpec((bb, None, HEAD_DIM, HEAD_DIM), lambda h, i: (i, h, 0, 0))],
        out_shape=[jax.ShapeDtypeStruct((Bs, RET_WIDTH), BF16),
                   jax.ShapeDtypeStruct(s0.shape, F32)],
        scratch_shapes=[pltpu.VMEM((bb, HEAD_DIM), F32)] * 4,
        compiler_params=_params("parallel", "parallel"),
        name="retention_sample",
    )(h_main, h_main, h_main, h_main, cos_t, sin_t, gn_w.reshape(1, -1), lg_tab, s0)


def _q_latent_kernel(q_ref, w_ref, o_ref):
    o_ref[...] = _dot_nt(q_ref[:, :NOPE_DIM], w_ref[...]).astype(o_ref.dtype)


def q_latent(Q, wuk_heads, *, row0, Bs):
    assert row0 % Bs == 0
    r0 = row0 // Bs
    return pl.pallas_call(
        _q_latent_kernel,
        grid=(MLA_HEADS,),
        in_specs=[pl.BlockSpec((Bs, QK_PAD), lambda h: (r0, h)),
                  pl.BlockSpec((None, KV_LORA, NOPE_DIM), lambda h: (h, 0, 0))],
        out_specs=pl.BlockSpec((Bs, KV_LORA), lambda h: (0, h)),
        out_shape=jax.ShapeDtypeStruct((Bs, MLA_HEADS * KV_LORA), BF16),
        compiler_params=_params("parallel"),
        name="q_latent",
    )(Q, wuk_heads)


PAGED_SLOTS = 4


def _paged_attn_kernel(pt_ref, ql_ref, qr_ref, cn_ref, rn_ref, cache_c, cache_r, o_ref,
                       cbuf, rbuf, sem, *, G, NC, Bs):
    b = pl.program_id(0)
    D = PAGED_SLOTS - 1

    def chunk_copies(page_ids, slot):
        cps = []
        for g, pid in enumerate(page_ids):
            rows = pl.ds(g * PAGE_SIZE, PAGE_SIZE)
            cps.append(pltpu.make_async_copy(cache_c.at[pid], cbuf.at[slot, rows, :], sem.at[0, slot]))
            cps.append(pltpu.make_async_copy(cache_r.at[pid], rbuf.at[slot, :, rows], sem.at[1, slot]))
        return cps

    def start_chunk(bb, c):
        ids = [pt_ref[bb, c * G + g] for g in range(G)]
        for cp in chunk_copies(ids, c % PAGED_SLOTS):
            cp.start()

    def wait_chunk(c):
        for cp in chunk_copies([0] * G, c % PAGED_SLOTS):
            cp.wait()

    @pl.when(b == 0)
    def _():
        for c in range(D):
            start_chunk(0, c)

    ql = ql_ref[...]
    qr = qr_ref[...]
    m = jnp.full((MLA_HEADS, 1), NEG_INF, F32)
    l = jnp.zeros((MLA_HEADS, 1), F32)
    acc = jnp.zeros((MLA_HEADS, KV_LORA), F32)
    for c in range(NC):
        nxt = c + D
        if nxt < NC:
            start_chunk(b, nxt)
        else:
            @pl.when(b + 1 < Bs)
            def _(nxt=nxt):
                start_chunk(b + 1, nxt - NC)
        wait_chunk(c)
        slot = c % PAGED_SLOTS
        cp = cbuf[slot].astype(BF16)
        s = (_dot_nt(ql, cp) + _dot(qr, rbuf[slot].astype(BF16))) * MLA_SCALE
        m_new = jnp.maximum(m, jnp.max(s, axis=-1, keepdims=True))
        alpha = jnp.exp(m - m_new)
        p = jnp.exp(s - m_new)
        l = l * alpha + jnp.sum(p, axis=-1, keepdims=True)
        acc = acc * alpha + _dot(p.astype(BF16), cp)
        m = m_new

    cn = cn_ref[...]
    s = (jnp.sum(ql.astype(F32) * cn, axis=-1, keepdims=True)
         + jnp.sum(qr.astype(F32) * rn_ref[...], axis=-1, keepdims=True)) * MLA_SCALE
    m2 = jnp.maximum(m, s)
    a2 = jnp.exp(m - m2)
    p = jnp.exp(s - m2)
    o_ref[...] = ((acc * a2 + p * cn) / (l * a2 + p)).astype(o_ref.dtype)


def paged_attention(page_table, q_lat, q_rope, c_new, kr_new, cache_c, cache_rt, *, G=16):
    Bs, n_pages = page_table.shape
    G = min(G, n_pages // PAGED_SLOTS)
    NC = n_pages // G
    assert n_pages % G == 0 and NC % PAGED_SLOTS == 0
    per_b = lambda d1, d2: pl.BlockSpec((None, d1, d2), lambda b, pt: (b, 0, 0))
    grid_spec = pltpu.PrefetchScalarGridSpec(
        num_scalar_prefetch=1,
        grid=(Bs,),
        in_specs=[per_b(MLA_HEADS, KV_LORA), per_b(MLA_HEADS, ROPE_DIM), per_b(1, KV_LORA), per_b(1, ROPE_DIM),
                  pl.BlockSpec(memory_space=pl.ANY), pl.BlockSpec(memory_space=pl.ANY)],
        out_specs=per_b(MLA_HEADS, KV_LORA),
        scratch_shapes=[pltpu.VMEM((PAGED_SLOTS, G * PAGE_SIZE, KV_LORA), F32),
                        pltpu.VMEM((PAGED_SLOTS, ROPE_DIM, G * PAGE_SIZE), F32),
                        pltpu.SemaphoreType.DMA((2, PAGED_SLOTS))],
    )
    return pl.pallas_call(
        functools.partial(_paged_attn_kernel, G=G, NC=NC, Bs=Bs),
        grid_spec=grid_spec,
        out_shape=jax.ShapeDtypeStruct((Bs, MLA_HEADS, KV_LORA), BF16),
        compiler_params=_params("arbitrary"),
        name="paged_attention",
    )(page_table, q_lat, q_rope, c_new.reshape(Bs, 1, KV_LORA), kr_new.reshape(Bs, 1, ROPE_DIM),
      cache_c, cache_rt)


def _v_up_kernel(o_ref, w_ref, out_ref):
    out_ref[...] = _dot(o_ref[...], w_ref[...]).astype(out_ref.dtype)


def v_up(o_lat2, wuv_heads):
    Bs = o_lat2.shape[0]
    return pl.pallas_call(
        _v_up_kernel,
        grid=(MLA_HEADS,),
        in_specs=[pl.BlockSpec((Bs, KV_LORA), lambda h: (0, h)),
                  pl.BlockSpec((None, KV_LORA, HEAD_DIM), lambda h: (h, 0, 0))],
        out_specs=pl.BlockSpec((Bs, HEAD_DIM), lambda h: (0, h)),
        out_shape=jax.ShapeDtypeStruct((Bs, MLA_HEADS * HEAD_DIM), BF16),
        compiler_params=_params("parallel"),
        name="v_up",
    )(o_lat2, wuv_heads)


def _prep_weights(w_in, w_q_up, w_uk, w_uv, w_o, w_down):
    c0 = Q_LORA + KV_LORA
    w_main = jnp.concatenate([w_in[:, :c0], w_in[:, c0 + ROPE_DIM:]], axis=1)
    w_kr = jnp.pad(w_in[:, c0:c0 + ROPE_DIM], ((0, 0), (0, 128 - ROPE_DIM))).astype(BF16)
    wq = w_q_up.reshape(Q_LORA, MLA_HEADS, NOPE_DIM + ROPE_DIM)
    wq_pad = jnp.pad(wq, ((0, 0), (0, 0), (0, QK_PAD - NOPE_DIM - ROPE_DIM))).reshape(Q_LORA, MLA_HEADS * QK_PAD).astype(BF16)
    wuk2 = w_uk.reshape(KV_LORA, MLA_HEADS * NOPE_DIM).astype(BF16)
    wuv2 = w_uv.reshape(KV_LORA, MLA_HEADS * HEAD_DIM).astype(BF16)
    wuk_h = jnp.transpose(w_uk, (1, 0, 2)).astype(BF16)
    wuv_h = jnp.transpose(w_uv, (1, 0, 2)).astype(BF16)
    return w_main, w_kr, wq_pad, wuk2, wuv2, wuk_h, wuv_h, w_o.astype(BF16), w_down.astype(BF16)


def kernel(x_prompt, x_sample, cache_kv_latent, cache_k_rope, state_retention, page_table,
           w_in, norm_attn, q_a_norm, w_q_up, kv_a_norm, w_uk, w_uv, ret_gn_w, w_o,
           norm_ffn, w_gate, w_up, w_down, norm_final):
    B, L, D = x_prompt.shape
    Bs, T, _ = x_sample.shape
    assert w_in.shape[0] == 1 and T == 1
    Rp = B * L
    R = Rp + Bs
    past = page_table.shape[1] * PAGE_SIZE
    (w_main, w_kr, wq_pad, wuk2, wuv2, wuk_h, wuv_h, wo, wd) = _prep_weights(
        w_in[0], w_q_up[0], w_uk[0], w_uv[0], w_o[0], w_down[0])
    lg = jnp.log1p(-jnp.exp2(-5.0 - jnp.arange(RET_HEADS, dtype=F32)))
    lg_tab = jnp.broadcast_to(lg[:, None, None], (RET_HEADS, 1, HEAD_DIM))

    pos_p = jnp.arange(L, dtype=jnp.int32)
    pos_s = jnp.full((Bs,), past, dtype=jnp.int32)
    cos64_p, sin64_p = _rope_tables(pos_p, ROPE_DIM // 2, 128)
    cos64_s, sin64_s = _rope_tables(pos_s, ROPE_DIM // 2, 128)
    cos64 = jnp.concatenate([jnp.tile(cos64_p, (B, 1)), cos64_s], axis=0)
    sin64 = jnp.concatenate([jnp.tile(sin64_p, (B, 1)), sin64_s], axis=0)
    cos128_p, sin128_p = _rope_tables(pos_p, HEAD_DIM // 2, 128)
    cos128_s, sin128_s = _rope_tables(pos_s, HEAD_DIM // 2, 128)

    xp = x_prompt.reshape(Rp, D)
    xs = x_sample.reshape(Bs, D)
    bm_all = _row_block(R, R // 8)
    bm_p = _row_block(Rp, 512)

    xn = rmsnorm_rows(xp, xs, norm_attn[0])
    h_main = matmul(xn, w_main, bm=bm_all, bn=512, name="in_proj")
    kr_raw = matmul(xn, w_kr, bm=bm_all, bn=128, name="in_proj_kr")
    Q, c_all, kr_all, K, V = mla_prep(h_main, kr_raw, cos64, sin64, q_a_norm[0], kv_a_norm[0],
                                      wq_pad, wuk2, wuv2, bm=_row_block(R, 320))
    c_p, c_s = c_all[:Rp], c_all[Rp:]
    kr_p, kr_s = kr_all[:Rp], kr_all[Rp:]

    mla_p = mla_prompt_attention(Q, K, V, B, L)
    ret_p, st_p = retention_prompt(h_main, cos128_p, sin128_p, ret_gn_w[0], lg_tab, B, L)

    q_lat = q_latent(Q, wuk_h, row0=Rp, Bs=Bs).reshape(Bs, MLA_HEADS, KV_LORA)
    q_rope = Q[Rp:].reshape(Bs, MLA_HEADS, QK_PAD)[:, :, NOPE_DIM:NOPE_DIM + ROPE_DIM]
    cache_rt = jnp.swapaxes(cache_k_rope[0], 1, 2)
    o_lat = paged_attention(page_table, q_lat, q_rope, c_s, kr_s, cache_kv_latent[0], cache_rt)
    mla_s = v_up(o_lat.reshape(Bs, MLA_HEADS * KV_LORA), wuv_h)
    ret_s, st_s = retention_sample(h_main, cos128_s, sin128_s, ret_gn_w[0], lg_tab, state_retention[0], row0=Rp)

    h_p, hn = out_proj(mla_p, ret_p, wo, xp, norm_ffn[0], xn, bm=bm_p, bn=1024)
    h_s, hn = out_proj(mla_s, ret_s, wo, xs, norm_ffn[0], hn, bm=Bs, bn=1024, hn_off=Rp // Bs)

    a = matmul(hn, w_gate[0], w2=w_up[0], bm=bm_all, bn=256, out_dtype=BF16, name="ffn_gate_up")
    out_p = matmul(a, wd, res=h_p, bm=bm_p, bn=512, m_rows=Rp, name="ffn_down")
    out_s = matmul(a, wd, res=h_s, bm=Bs, bn=512, m_rows=Bs, m_off=Rp // Bs, name="ffn_down")
    y_p = rmsnorm(out_p, norm_final, F32)
    y_s = rmsnorm(out_s, norm_final, F32)

    return (y_p.reshape(B, L, D), y_s.reshape(Bs, T, D),
            c_p.reshape(1, B, L, KV_LORA), kr_p.reshape(1, B, L, ROPE_DIM), st_p[None],
            c_s.reshape(1, Bs, T, KV_LORA), kr_s.reshape(1, Bs, T, ROPE_DIM), st_s[None])
```

```python
import functools
import math

import jax
import jax.numpy as jnp
from jax import lax
from jax.experimental import pallas as pl
from jax.experimental.pallas import tpu as pltpu

F32 = jnp.float32
BF16 = jnp.bfloat16

HEAD_DIM = 128
MLA_HEADS = 16
RET_HEADS = 16
Q_LORA = 1536
KV_LORA = 512
NOPE_DIM = 128
ROPE_DIM = 64
RET_WIDTH = RET_HEADS * HEAD_DIM
QK_PAD = 256
MLA_SCALE = (NOPE_DIM + ROPE_DIM) ** -0.5
ROPE_BASE = 10000.0
RMS_EPS = 1e-6
GN_EPS = 1e-6
NEG_INF = -1e30
PAGE_SIZE = 128
BF16_ROWS = 16

VMEM_LIMIT = 56 * 1024 * 1024


def _params(*sem):
    return pltpu.CompilerParams(dimension_semantics=sem, vmem_limit_bytes=VMEM_LIMIT)


def _row_block(rows, target):
    best = None
    for d in range(BF16_ROWS, min(rows, target) + 1, BF16_ROWS):
        if rows % d == 0:
            best = d
    assert best is not None, rows
    return best


def _rms(x, g):
    ms = jnp.mean(x * x, axis=-1, keepdims=True)
    return x * lax.rsqrt(ms + RMS_EPS) * g


def _silu(x):
    return x * (1.0 / (1.0 + jnp.exp(-x)))


def _dot(a, b):
    return jnp.dot(a, b, preferred_element_type=F32)


def _dot_nt(a, b):
    return lax.dot_general(a, b, (((1,), (1,)), ((), ())), preferred_element_type=F32)


def _dot_tn(a, b):
    return lax.dot_general(a, b, (((0,), (0,)), ((), ())), preferred_element_type=F32)


def _rmsnorm_kernel(x_ref, g_ref, o_ref):
    o_ref[...] = _rms(x_ref[...], g_ref[...]).astype(o_ref.dtype)


def rmsnorm(x, g, out_dtype, br=256):
    R, D = x.shape
    br = min(br, R)
    return pl.pallas_call(
        _rmsnorm_kernel,
        grid=(R // br,),
        in_specs=[pl.BlockSpec((br, D), lambda i: (i, 0)),
                  pl.BlockSpec((1, D), lambda i: (0, 0))],
        out_specs=pl.BlockSpec((br, D), lambda i: (i, 0)),
        out_shape=jax.ShapeDtypeStruct((R, D), out_dtype),
        compiler_params=_params("parallel"),
        name="rmsnorm",
    )(x, g.reshape(1, D))


def _rmsnorm2_kernel(xp_ref, xs_ref, g_ref, o_ref, *, npb):
    i = pl.program_id(0)

    @pl.when(i < npb)
    def _():
        o_ref[...] = _rms(xp_ref[...], g_ref[...]).astype(o_ref.dtype)

    @pl.when(i >= npb)
    def _():
        o_ref[...] = _rms(xs_ref[...], g_ref[...]).astype(o_ref.dtype)


def rmsnorm_rows(xp, xs, g):
    Rp, D = xp.shape
    Rs = xs.shape[0]
    assert Rp % Rs == 0
    npb = Rp // Rs
    return pl.pallas_call(
        functools.partial(_rmsnorm2_kernel, npb=npb),
        grid=(npb + 1,),
        in_specs=[pl.BlockSpec((Rs, D), lambda i: (jnp.minimum(i, npb - 1), 0)),
                  pl.BlockSpec((Rs, D), lambda i: (0, 0)),
                  pl.BlockSpec((1, D), lambda i: (0, 0))],
        out_specs=pl.BlockSpec((Rs, D), lambda i: (i, 0)),
        out_shape=jax.ShapeDtypeStruct((Rp + Rs, D), BF16),
        compiler_params=_params("arbitrary"),
        name="rmsnorm_rows",
    )(xp, xs, g.reshape(1, D))


def _mm_kernel(*refs, mode):
    if mode == "glu":
        x_ref, w_ref, w2_ref, o_ref = refs
        x = x_ref[...]
        g = _dot(x, w_ref[...].astype(BF16))
        u = _dot(x, w2_ref[...].astype(BF16))
        o_ref[...] = (_silu(g) * u).astype(o_ref.dtype)
    elif mode == "res":
        x_ref, w_ref, r_ref, o_ref = refs
        o_ref[...] = (r_ref[...] + _dot(x_ref[...], w_ref[...].astype(BF16))).astype(o_ref.dtype)
    else:
        x_ref, w_ref, o_ref = refs
        o_ref[...] = _dot(x_ref[...], w_ref[...].astype(BF16)).astype(o_ref.dtype)


def matmul(x, w, *, bm, bn, out_dtype=F32, res=None, w2=None, m_rows=None, m_off=0, name="matmul"):
    K = x.shape[1]
    N = w.shape[1]
    M = x.shape[0] if m_rows is None else m_rows
    bn = min(bn, N)
    mode = "glu" if w2 is not None else ("res" if res is not None else "plain")
    assert M % bm == 0 and N % bn == 0 and bn % 128 == 0
    in_specs = [pl.BlockSpec((bm, K), lambda i, j: (i + m_off, 0)),
                pl.BlockSpec((K, bn), lambda i, j: (0, j))]
    args = [x, w]
    if mode == "glu":
        in_specs.append(pl.BlockSpec((K, bn), lambda i, j: (0, j)))
        args.append(w2)
    if mode == "res":
        in_specs.append(pl.BlockSpec((bm, bn), lambda i, j: (i, j)))
        args.append(res)
    return pl.pallas_call(
        functools.partial(_mm_kernel, mode=mode),
        grid=(M // bm, N // bn),
        in_specs=in_specs,
        out_specs=pl.BlockSpec((bm, bn), lambda i, j: (i, j)),
        out_shape=jax.ShapeDtypeStruct((M, N), out_dtype),
        compiler_params=_params("parallel", "parallel"),
        name=name,
    )(*args)


def _out_proj_kernel(xa_ref, xb_ref, wa_ref, wb_ref, r_ref, g_ref, hn_in_ref, h_ref, hn_ref, row_scr,
                     *, nj, bn):
    del hn_in_ref
    j = pl.program_id(1)
    h = r_ref[...] + (_dot(xa_ref[...], wa_ref[...]) + _dot(xb_ref[...], wb_ref[...]))
    h_ref[...] = h
    row_scr[j] = h

    @pl.when(j == nj - 1)
    def _():
        n_cols = nj * bn
        ss = None
        for jj in range(nj):
            hj = row_scr[jj]
            s = jnp.sum(hj * hj, axis=-1, keepdims=True)
            ss = s if ss is None else ss + s
        inv = lax.rsqrt(ss / n_cols + RMS_EPS)
        for jj in range(nj):
            c = slice(jj * bn, (jj + 1) * bn)
            hn_ref[:, c] = (row_scr[jj] * inv * g_ref[:, c]).astype(hn_ref.dtype)


def out_proj(xa, xb, w, res, g, hn_buf, *, bm, bn, hn_off=0):
    M, Ka = xa.shape
    N = w.shape[1]
    nj = N // bn
    assert xb.shape[1] == Ka and w.shape[0] == 2 * Ka and M % bm == 0 and N % bn == 0
    assert hn_buf.shape[1] == N and hn_buf.dtype == BF16
    return pl.pallas_call(
        functools.partial(_out_proj_kernel, nj=nj, bn=bn),
        grid=(M // bm, nj),
        in_specs=[pl.BlockSpec((bm, Ka), lambda i, j: (i, 0)),
                  pl.BlockSpec((bm, Ka), lambda i, j: (i, 0)),
                  pl.BlockSpec((Ka, bn), lambda i, j: (0, j)),
                  pl.BlockSpec((Ka, bn), lambda i, j: (1, j)),
                  pl.BlockSpec((bm, bn), lambda i, j: (i, j)),
                  pl.BlockSpec((1, N), lambda i, j: (0, 0)),
                  pl.BlockSpec(memory_space=pl.ANY)],
        out_specs=[pl.BlockSpec((bm, bn), lambda i, j: (i, j)),
                   pl.BlockSpec((bm, N), lambda i, j: (i + hn_off, 0))],
        out_shape=[jax.ShapeDtypeStruct((M, N), F32),
                   jax.ShapeDtypeStruct(hn_buf.shape, BF16)],
        scratch_shapes=[pltpu.VMEM((nj, bm, bn), F32)],
        input_output_aliases={6: 1},
        compiler_params=_params("parallel", "arbitrary"),
        name="out_proj",
    )(xa, xb, w, w, res, g.reshape(1, N), hn_buf)


def _rope_tables(pos, half, width):
    inv = jnp.exp(-math.log(ROPE_BASE) * jnp.arange(half, dtype=F32) / half)
    ang = pos.astype(F32)[:, None] * inv[None, :]
    cos, sin = jnp.cos(ang), jnp.sin(ang)
    n = pos.shape[0]
    pad = width - 2 * half
    cos_t = jnp.concatenate([cos, cos, jnp.ones((n, pad), F32)], axis=-1)
    sin_t = jnp.concatenate([-sin, sin, jnp.zeros((n, pad), F32)], axis=-1)
    return cos_t, sin_t


def _rope64_tile(t, cos, sin):
    lane = lax.broadcasted_iota(jnp.int32, t.shape, 1)
    other = jnp.where(lane < ROPE_DIM // 2, pltpu.roll(t, 128 - ROPE_DIM // 2, axis=1),
                      pltpu.roll(t, ROPE_DIM // 2, axis=1))
    return t * cos + other * sin


def _rope128_tile(t, cos, sin):
    return t * cos + pltpu.roll(t, HEAD_DIM // 2, axis=1) * sin


def _mla_prep_kernel(h_ref, kr_ref, cos_ref, sin_ref, gq_ref, gkv_ref, wq_ref, wuk_ref, wuv_ref,
                     q_ref, c_ref, kro_ref, k_ref, v_ref):
    cos, sin = cos_ref[...], sin_ref[...]
    qn = _rms(h_ref[:, :Q_LORA], gq_ref[...]).astype(BF16)
    q = _dot(qn, wq_ref[...])
    for hh in range(MLA_HEADS):
        o = hh * QK_PAD
        q_ref[:, o:o + NOPE_DIM] = q[:, o:o + NOPE_DIM].astype(BF16)
        q_ref[:, o + NOPE_DIM:o + QK_PAD] = _rope64_tile(q[:, o + NOPE_DIM:o + QK_PAD], cos, sin).astype(BF16)
    c = _rms(h_ref[:, Q_LORA:Q_LORA + KV_LORA], gkv_ref[...])
    c_ref[...] = c
    kr = _rope64_tile(kr_ref[...], cos, sin)
    kro_ref[...] = kr[:, :ROPE_DIM]
    cb = c.astype(BF16)
    krb = kr.astype(BF16)
    kn = _dot(cb, wuk_ref[...])
    for hh in range(MLA_HEADS):
        o = hh * QK_PAD
        k_ref[:, o:o + NOPE_DIM] = kn[:, hh * NOPE_DIM:(hh + 1) * NOPE_DIM].astype(BF16)
        k_ref[:, o + NOPE_DIM:o + QK_PAD] = krb
    v_ref[...] = _dot(cb, wuv_ref[...]).astype(BF16)


def mla_prep(h_main, kr_raw, cos_t, sin_t, gq, gkv, wq_pad, wuk2, wuv2, *, bm):
    R = h_main.shape[0]
    assert R % bm == 0
    row = lambda i: (i, 0)
    const = lambda i: (0, 0)
    return pl.pallas_call(
        _mla_prep_kernel,
        grid=(R // bm,),
        in_specs=[pl.BlockSpec((bm, Q_LORA + KV_LORA), row),
                  pl.BlockSpec((bm, 128), row),
                  pl.BlockSpec((bm, 128), row),
                  pl.BlockSpec((bm, 128), row),
                  pl.BlockSpec((1, Q_LORA), const),
                  pl.BlockSpec((1, KV_LORA), const),
                  pl.BlockSpec(wq_pad.shape, const, pipeline_mode=pl.Buffered(1)),
                  pl.BlockSpec(wuk2.shape, const),
                  pl.BlockSpec(wuv2.shape, const)],
        out_specs=[pl.BlockSpec((bm, MLA_HEADS * QK_PAD), row),
                   pl.BlockSpec((bm, KV_LORA), row),
                   pl.BlockSpec((bm, ROPE_DIM), row),
                   pl.BlockSpec((bm, MLA_HEADS * QK_PAD), row),
                   pl.BlockSpec((bm, MLA_HEADS * HEAD_DIM), row)],
        out_shape=[jax.ShapeDtypeStruct((R, MLA_HEADS * QK_PAD), BF16),
                   jax.ShapeDtypeStruct((R, KV_LORA), F32),
                   jax.ShapeDtypeStruct((R, ROPE_DIM), F32),
                   jax.ShapeDtypeStruct((R, MLA_HEADS * QK_PAD), BF16),
                   jax.ShapeDtypeStruct((R, MLA_HEADS * HEAD_DIM), BF16)],
        compiler_params=_params("parallel"),
        name="mla_prep",
    )(h_main, kr_raw, cos_t, sin_t, gq.reshape(1, -1), gkv.reshape(1, -1), wq_pad, wuk2, wuv2)


def _causal_attn_kernel(q_ref, k_ref, v_ref, o_ref, *, L, bq):
    tri = (lax.broadcasted_iota(jnp.int32, (bq, bq), 0) >= lax.broadcasted_iota(jnp.int32, (bq, bq), 1))
    for qi in range(L // bq):
        lo, hi = qi * bq, (qi + 1) * bq
        q = q_ref[lo:hi, :]
        s_d = jnp.where(tri, _dot_nt(q, k_ref[lo:hi, :]) * MLA_SCALE, NEG_INF)
        m = jnp.max(s_d, axis=-1, keepdims=True)
        if qi > 0:
            s_p = _dot_nt(q, k_ref[:lo, :]) * MLA_SCALE
            m = jnp.maximum(m, jnp.max(s_p, axis=-1, keepdims=True))
        p_d = jnp.exp(s_d - m)
        l = jnp.sum(p_d, axis=-1, keepdims=True)
        acc = _dot(p_d.astype(BF16), v_ref[lo:hi, :])
        if qi > 0:
            p_p = jnp.exp(s_p - m)
            l = l + jnp.sum(p_p, axis=-1, keepdims=True)
            acc = acc + _dot(p_p.astype(BF16), v_ref[:lo, :])
        o_ref[lo:hi, :] = (acc / l).astype(o_ref.dtype)


def mla_prompt_attention(Q, K, V, B, L, *, bq=512):
    bq = min(bq, L)
    return pl.pallas_call(
        functools.partial(_causal_attn_kernel, L=L, bq=bq),
        grid=(B, MLA_HEADS),
        in_specs=[pl.BlockSpec((L, QK_PAD), lambda b, h: (b, h)),
                  pl.BlockSpec((L, QK_PAD), lambda b, h: (b, h)),
                  pl.BlockSpec((L, HEAD_DIM), lambda b, h: (b, h))],
        out_specs=pl.BlockSpec((L, HEAD_DIM), lambda b, h: (b, h)),
        out_shape=jax.ShapeDtypeStruct((B * L, MLA_HEADS * HEAD_DIM), BF16),
        compiler_params=_params("parallel", "parallel"),
        name="mla_prompt_attention",
    )(Q, K, V)


def _group_norm_gate(o, rg, gw):
    mu = jnp.mean(o, axis=-1, keepdims=True)
    d = o - mu
    var = jnp.mean(d * d, axis=-1, keepdims=True)
    return _silu(rg) * (d * lax.rsqrt(var + GN_EPS) * gw)


def _ret_prompt_kernel(rq_ref, rk_ref, rv_ref, rg_ref, cos_ref, sin_ref, gw_ref, lg_ref,
                       o_ref, s_ref, *, L, C):
    lg = lg_ref[0, :, :1]
    ii = lax.broadcasted_iota(jnp.int32, (C, C), 0)
    jj = lax.broadcasted_iota(jnp.int32, (C, C), 1)
    diff = (ii - jj).astype(F32)
    dec = jnp.where(diff >= 0, jnp.exp(jnp.maximum(diff, 0.0) * lg), 0.0)
    idx = lax.broadcasted_iota(jnp.int32, (C, 1), 0).astype(F32)
    q_dec = jnp.exp((idx + 1.0) * lg)
    k_dec = jnp.exp((C - 1.0 - idx) * lg)
    s_dec = jnp.exp(C * lg)
    gw = gw_ref[...]
    S = jnp.zeros((HEAD_DIM, HEAD_DIM), F32)
    for n in range(L // C):
        r = slice(n * C, (n + 1) * C)
        cos, sin = cos_ref[r, :], sin_ref[r, :]
        qc = _rope128_tile(rq_ref[r, :], cos, sin)
        kc = _rope128_tile(rk_ref[r, :], cos, sin) * (HEAD_DIM ** -0.5)
        qb = qc.astype(BF16)
        vb = rv_ref[r, :].astype(BF16)
        sc = _dot_nt(qb, kc.astype(BF16)) * dec
        o = _dot(sc.astype(BF16), vb) + _dot(qb, S.astype(BF16)) * q_dec
        S = S * s_dec + _dot_tn((kc * k_dec).astype(BF16), vb)
        o_ref[r, :] = _group_norm_gate(o, rg_ref[r, :], gw).astype(o_ref.dtype)
    s_ref[...] = S


def retention_prompt(h_main, cos_t, sin_t, gn_w, lg_tab, B, L, *, C=256):
    C = min(C, L)
    nb = RET_WIDTH // HEAD_DIM
    base = (Q_LORA + KV_LORA) // HEAD_DIM
    blk = lambda off: pl.BlockSpec((L, HEAD_DIM), lambda b, h, off=off: (b, off + h))
    return pl.pallas_call(
        functools.partial(_ret_prompt_kernel, L=L, C=C),
        grid=(B, RET_HEADS),
        in_specs=[blk(base), blk(base + nb), blk(base + 2 * nb), blk(base + 3 * nb),
                  pl.BlockSpec((L, HEAD_DIM), lambda b, h: (0, 0)),
                  pl.BlockSpec((L, HEAD_DIM), lambda b, h: (0, 0)),
                  pl.BlockSpec((1, HEAD_DIM), lambda b, h: (0, h)),
                  pl.BlockSpec((1, 1, HEAD_DIM), lambda b, h: (h, 0, 0))],
        out_specs=[pl.BlockSpec((L, HEAD_DIM), lambda b, h: (b, h)),
                   pl.BlockSpec((None, None, HEAD_DIM, HEAD_DIM), lambda b, h: (b, h, 0, 0))],
        out_shape=[jax.ShapeDtypeStruct((B * L, RET_WIDTH), BF16),
                   jax.ShapeDtypeStruct((B, RET_HEADS, HEAD_DIM, HEAD_DIM), F32)],
        compiler_params=_params("parallel", "parallel"),
        name="retention_prompt",
    )(h_main, h_main, h_main, h_main, cos_t, sin_t, gn_w.reshape(1, -1), lg_tab)


def _ret_sample_kernel(rq_ref, rk_ref, rv_ref, rg_ref, cos_ref, sin_ref, gw_ref, lg_ref, s0_ref,
                       o_ref, s_ref, q_scr, k_scr, v_scr, oc_scr, *, bb):
    gamma = jnp.exp(lg_ref[0, :, :1])
    cos, sin = cos_ref[...], sin_ref[...]
    q = _rope128_tile(rq_ref[...], cos, sin)
    k = _rope128_tile(rk_ref[...], cos, sin) * (HEAD_DIM ** -0.5)
    v = rv_ref[...]
    q_scr[...] = q
    k_scr[...] = k
    v_scr[...] = v
    eye = (lax.broadcasted_iota(jnp.int32, (HEAD_DIM, HEAD_DIM), 0)
           == lax.broadcasted_iota(jnp.int32, (HEAD_DIM, HEAD_DIM), 1))

    def body(b, carry):
        s0 = s0_ref[b]
        qb = jnp.broadcast_to(q_scr[pl.ds(b, 1), :], (8, HEAD_DIM)).astype(BF16)
        oc_scr[pl.ds(b, 1), :] = _dot(qb, s0.astype(BF16))[:1, :]
        kd = jnp.where(eye, jnp.broadcast_to(k_scr[pl.ds(b, 1), :], (HEAD_DIM, HEAD_DIM)), 0.0)
        vr = jnp.broadcast_to(v_scr[pl.ds(b, 1), :], (HEAD_DIM, HEAD_DIM))
        s_ref[b] = s0 * gamma + _dot(kd.astype(BF16), vr.astype(BF16))
        return carry

    lax.fori_loop(0, bb, body, 0, unroll=min(8, bb))
    qk = jnp.sum(q * k, axis=-1, keepdims=True)
    o = qk * v + oc_scr[...] * gamma
    o_ref[...] = _group_norm_gate(o, rg_ref[...], gw_ref[...]).astype(o_ref.dtype)


def retention_sample(h_main, cos_t, sin_t, gn_w, lg_tab, s0, *, row0, bb=32):
    Bs = s0.shape[0]
    bb = min(bb, Bs)
    assert Bs % bb == 0 and row0 % bb == 0
    r0 = row0 // bb
    nb = RET_WIDTH // HEAD_DIM
    base = (Q_LORA + KV_LORA) // HEAD_DIM
    blk = lambda off: pl.BlockSpec((bb, HEAD_DIM), lambda h, i, off=off: (r0 + i, off + h))
    return pl.pallas_call(
        functools.partial(_ret_sample_kernel, bb=bb),
        grid=(RET_HEADS, Bs // bb),
        in_specs=[blk(base), blk(base + nb), blk(base + 2 * nb), blk(base + 3 * nb),
                  pl.BlockSpec((bb, HEAD_DIM), lambda h, i: (i, 0)),
                  pl.BlockSpec((bb, HEAD_DIM), lambda h, i: (i, 0)),
                  pl.BlockSpec((1, HEAD_DIM), lambda h, i: (0, h)),
                  pl.BlockSpec((1, 1, HEAD_DIM), lambda h, i: (h, 0, 0)),
                  pl.BlockSpec((bb, None, HEAD_DIM, HEAD_DIM), lambda h, i: (i, h, 0, 0))],
        out_specs=[pl.BlockSpec((bb, HEAD_DIM), lambda h, i: (i, h)),
                   pl.BlockSpec((bb, None, HEAD_DIM, HEAD_DIM), lambda h, i: (i, h, 0, 0))],
        out_shape=[jax.ShapeDtypeStruct((Bs, RET_WIDTH), BF16),
                   jax.ShapeDtypeStruct(s0.shape, F32)],
        scratch_shapes=[pltpu.VMEM((bb, HEAD_DIM), F32)] * 4,
        compiler_params=_params("parallel", "parallel"),
        name="retention_sample",
    )(h_main, h_main, h_main, h_main, cos_t, sin_t, gn_w.reshape(1, -1), lg_tab, s0)


def _q_latent_kernel(q_ref, w_ref, o_ref):
    o_ref[...] = _dot_nt(q_ref[:, :NOPE_DIM], w_ref[...]).astype(o_ref.dtype)


def q_latent(Q, wuk_heads, *, row0, Bs):
    assert row0 % Bs == 0
    r0 = row0 // Bs
    return pl.pallas_call(
        _q_latent_kernel,
        grid=(MLA_HEADS,),
        in_specs=[pl.BlockSpec((Bs, QK_PAD), lambda h: (r0, h)),
                  pl.BlockSpec((None, KV_LORA, NOPE_DIM), lambda h: (h, 0, 0))],
        out_specs=pl.BlockSpec((Bs, KV_LORA), lambda h: (0, h)),
        out_shape=jax.ShapeDtypeStruct((Bs, MLA_HEADS * KV_LORA), BF16),
        compiler_params=_params("parallel"),
        name="q_latent",
    )(Q, wuk_heads)


PAGED_SLOTS = 4


def _paged_attn_kernel(pt_ref, ql_ref, qr_ref, cn_ref, rn_ref, cache_c, cache_r, o_ref,
                       cbuf, rbuf, sem, *, G, NC, Bs):
    b = pl.program_id(0)
    D = PAGED_SLOTS - 1

    def chunk_copies(page_ids, slot):
        cps = []
        for g, pid in enumerate(page_ids):
            rows = pl.ds(g * PAGE_SIZE, PAGE_SIZE)
            cps.append(pltpu.make_async_copy(cache_c.at[pid], cbuf.at[slot, rows, :], sem.at[0, slot]))
            cps.append(pltpu.make_async_copy(cache_r.at[pid], rbuf.at[slot, :, rows], sem.at[1, slot]))
        return cps

    def start_chunk(bb, c):
        ids = [pt_ref[bb, c * G + g] for g in range(G)]
        for cp in chunk_copies(ids, c % PAGED_SLOTS):
            cp.start()

    def wait_chunk(c):
        for cp in chunk_copies([0] * G, c % PAGED_SLOTS):
            cp.wait()

    @pl.when(b == 0)
    def _():
        for c in range(D):
            start_chunk(0, c)

    ql = ql_ref[...]
    qr = qr_ref[...]
    m = jnp.full((MLA_HEADS, 1), NEG_INF, F32)
    l = jnp.zeros((MLA_HEADS, 1), F32)
    acc = jnp.zeros((MLA_HEADS, KV_LORA), F32)
    for c in range(NC):
        nxt = c + D
        if nxt < NC:
            start_chunk(b, nxt)
        else:
            @pl.when(b + 1 < Bs)
            def _(nxt=nxt):
                start_chunk(b + 1, nxt - NC)
        wait_chunk(c)
        slot = c % PAGED_SLOTS
        cp = cbuf[slot].astype(BF16)
        s = (_dot_nt(ql, cp) + _dot(qr, rbuf[slot].astype(BF16))) * MLA_SCALE
        m_new = jnp.maximum(m, jnp.max(s, axis=-1, keepdims=True))
        alpha = jnp.exp(m - m_new)
        p = jnp.exp(s - m_new)
        l = l * alpha + jnp.sum(p, axis=-1, keepdims=True)
        acc = acc * alpha + _dot(p.astype(BF16), cp)
        m = m_new

    cn = cn_ref[...]
    s = (jnp.sum(ql.astype(F32) * cn, axis=-1, keepdims=True)
         + jnp.sum(qr.astype(F32) * rn_ref[...], axis=-1, keepdims=True)) * MLA_SCALE
    m2 = jnp.maximum(m, s)
    a2 = jnp.exp(m - m2)
    p = jnp.exp(s - m2)
    o_ref[...] = ((acc * a2 + p * cn) / (l * a2 + p)).astype(o_ref.dtype)


def paged_attention(page_table, q_lat, q_rope, c_new, kr_new, cache_c, cache_rt, *, G=16):
    Bs, n_pages = page_table.shape
    G = min(G, n_pages // PAGED_SLOTS)
    NC = n_pages // G
    assert n_pages % G == 0 and NC % PAGED_SLOTS == 0
    per_b = lambda d1, d2: pl.BlockSpec((None, d1, d2), lambda b, pt: (b, 0, 0))
    grid_spec = pltpu.PrefetchScalarGridSpec(
        num_scalar_prefetch=1,
        grid=(Bs,),
        in_specs=[per_b(MLA_HEADS, KV_LORA), per_b(MLA_HEADS, ROPE_DIM), per_b(1, KV_LORA), per_b(1, ROPE_DIM),
                  pl.BlockSpec(memory_space=pl.ANY), pl.BlockSpec(memory_space=pl.ANY)],
        out_specs=per_b(MLA_HEADS, KV_LORA),
        scratch_shapes=[pltpu.VMEM((PAGED_SLOTS, G * PAGE_SIZE, KV_LORA), F32),
                        pltpu.VMEM((PAGED_SLOTS, ROPE_DIM, G * PAGE_SIZE), F32),
                        pltpu.SemaphoreType.DMA((2, PAGED_SLOTS))],
    )
    return pl.pallas_call(
        functools.partial(_paged_attn_kernel, G=G, NC=NC, Bs=Bs),
        grid_spec=grid_spec,
        out_shape=jax.ShapeDtypeStruct((Bs, MLA_HEADS, KV_LORA), BF16),
        compiler_params=_params("arbitrary"),
        name="paged_attention",
    )(page_table, q_lat, q_rope, c_new.reshape(Bs, 1, KV_LORA), kr_new.reshape(Bs, 1, ROPE_DIM),
      cache_c, cache_rt)


def _v_up_kernel(o_ref, w_ref, out_ref):
    out_ref[...] = _dot(o_ref[...], w_ref[...]).astype(out_ref.dtype)


def v_up(o_lat2, wuv_heads):
    Bs = o_lat2.shape[0]
    return pl.pallas_call(
        _v_up_kernel,
        grid=(MLA_HEADS,),
        in_specs=[pl.BlockSpec((Bs, KV_LORA), lambda h: (0, h)),
                  pl.BlockSpec((None, KV_LORA, HEAD_DIM), lambda h: (h, 0, 0))],
        out_specs=pl.BlockSpec((Bs, HEAD_DIM), lambda h: (0, h)),
        out_shape=jax.ShapeDtypeStruct((Bs, MLA_HEADS * HEAD_DIM), BF16),
        compiler_params=_params("parallel"),
        name="v_up",
    )(o_lat2, wuv_heads)


REPACK_BN = 512


def _repack_kernel(a_ref, b_ref, o_ref, *, first_shifted):
    j = pl.program_id(0)

    @pl.when(j < first_shifted)
    def _():
        o_ref[...] = a_ref[...].astype(BF16)

    @pl.when(j >= first_shifted)
    def _():
        nt = REPACK_BN // 128
        tiles = [a_ref[:, t * 128:(t + 1) * 128] for t in range(nt)] + [b_ref[...]]
        swapped = [pltpu.roll(t, ROPE_DIM, axis=1) for t in tiles]
        low = lax.broadcasted_iota(jnp.int32, tiles[0].shape, 1) < 128 - ROPE_DIM
        for t in range(nt):
            o_ref[:, t * 128:(t + 1) * 128] = jnp.where(low, swapped[t], swapped[t + 1]).astype(BF16)


def repack_w_in(w_in):
    K, n_in = w_in.shape
    c0 = Q_LORA + KV_LORA
    n_out = n_in - ROPE_DIM
    assert ROPE_DIM == 64 and c0 % REPACK_BN == 0 and n_out % REPACK_BN == 0
    per = REPACK_BN // 128
    return pl.pallas_call(
        functools.partial(_repack_kernel, first_shifted=c0 // REPACK_BN),
        grid=(n_out // REPACK_BN,),
        in_specs=[pl.BlockSpec((K, REPACK_BN), lambda j: (0, j)),
                  pl.BlockSpec((K, 128), lambda j: (0, (j + 1) * per))],
        out_specs=pl.BlockSpec((K, REPACK_BN), lambda j: (0, j)),
        out_shape=jax.ShapeDtypeStruct((K, n_out), BF16),
        compiler_params=_params("parallel"),
        name="repack_w_in",
    )(w_in, w_in)


def _prep_weights(w_in, w_q_up, w_uk, w_uv, w_o, w_down):
    c0 = Q_LORA + KV_LORA
    w_main = repack_w_in(w_in)
    w_kr = jnp.pad(w_in[:, c0:c0 + ROPE_DIM], ((0, 0), (0, 128 - ROPE_DIM))).astype(BF16)
    wq = w_q_up.reshape(Q_LORA, MLA_HEADS, NOPE_DIM + ROPE_DIM)
    wq_pad = jnp.pad(wq, ((0, 0), (0, 0), (0, QK_PAD - NOPE_DIM - ROPE_DIM))).reshape(Q_LORA, MLA_HEADS * QK_PAD).astype(BF16)
    wuk2 = w_uk.reshape(KV_LORA, MLA_HEADS * NOPE_DIM).astype(BF16)
    wuv2 = w_uv.reshape(KV_LORA, MLA_HEADS * HEAD_DIM).astype(BF16)
    wuk_h = jnp.transpose(w_uk, (1, 0, 2)).astype(BF16)
    wuv_h = jnp.transpose(w_uv, (1, 0, 2)).astype(BF16)
    return w_main, w_kr, wq_pad, wuk2, wuv2, wuk_h, wuv_h, w_o.astype(BF16), w_down.astype(BF16)


def kernel(x_prompt, x_sample, cache_kv_latent, cache_k_rope, state_retention, page_table,
           w_in, norm_attn, q_a_norm, w_q_up, kv_a_norm, w_uk, w_uv, ret_gn_w, w_o,
           norm_ffn, w_gate, w_up, w_down, norm_final):
    B, L, D = x_prompt.shape
    Bs, T, _ = x_sample.shape
    assert w_in.shape[0] == 1 and T == 1
    Rp = B * L
    R = Rp + Bs
    past = page_table.shape[1] * PAGE_SIZE
    (w_main, w_kr, wq_pad, wuk2, wuv2, wuk_h, wuv_h, wo, wd) = _prep_weights(
        w_in[0], w_q_up[0], w_uk[0], w_uv[0], w_o[0], w_down[0])
    lg = jnp.log1p(-jnp.exp2(-5.0 - jnp.arange(RET_HEADS, dtype=F32)))
    lg_tab = jnp.broadcast_to(lg[:, None, None], (RET_HEADS, 1, HEAD_DIM))

    pos_p = jnp.arange(L, dtype=jnp.int32)
    pos_s = jnp.full((Bs,), past, dtype=jnp.int32)
    cos64_p, sin64_p = _rope_tables(pos_p, ROPE_DIM // 2, 128)
    cos64_s, sin64_s = _rope_tables(pos_s, ROPE_DIM // 2, 128)
    cos64 = jnp.concatenate([jnp.tile(cos64_p, (B, 1)), cos64_s], axis=0)
    sin64 = jnp.concatenate([jnp.tile(sin64_p, (B, 1)), sin64_s], axis=0)
    cos128_p, sin128_p = _rope_tables(pos_p, HEAD_DIM // 2, 128)
    cos128_s, sin128_s = _rope_tables(pos_s, HEAD_DIM // 2, 128)

    xp = x_prompt.reshape(Rp, D)
    xs = x_sample.reshape(Bs, D)
    bm_all = _row_block(R, R // 8)
    bm_p = _row_block(Rp, 512)

    xn = rmsnorm_rows(xp, xs, norm_attn[0])
    h_main = matmul(xn, w_main, bm=bm_all, bn=512, name="in_proj")
    kr_raw = matmul(xn, w_kr, bm=bm_all, bn=128, name="in_proj_kr")
    Q, c_all, kr_all, K, V = mla_prep(h_main, kr_raw, cos64, sin64, q_a_norm[0], kv_a_norm[0],
                                      wq_pad, wuk2, wuv2, bm=_row_block(R, 320))
    c_p, c_s = c_all[:Rp], c_all[Rp:]
    kr_p, kr_s = kr_all[:Rp], kr_all[Rp:]

    mla_p = mla_prompt_attention(Q, K, V, B, L)
    ret_p, st_p = retention_prompt(h_main, cos128_p, sin128_p, ret_gn_w[0], lg_tab, B, L)

    q_lat = q_latent(Q, wuk_h, row0=Rp, Bs=Bs).reshape(Bs, MLA_HEADS, KV_LORA)
    q_rope = Q[Rp:].reshape(Bs, MLA_HEADS, QK_PAD)[:, :, NOPE_DIM:NOPE_DIM + ROPE_DIM]
    cache_rt = jnp.swapaxes(cache_k_rope[0], 1, 2)
    o_lat = paged_attention(page_table, q_lat, q_rope, c_s, kr_s, cache_kv_latent[0], cache_rt)
    mla_s = v_up(o_lat.reshape(Bs, MLA_HEADS * KV_LORA), wuv_h)
    ret_s, st_s = retention_sample(h_main, cos128_s, sin128_s, ret_gn_w[0], lg_tab, state_retention[0], row0=Rp)

    h_p, hn = out_proj(mla_p, ret_p, wo, xp, norm_ffn[0], xn, bm=bm_p, bn=1024)
    h_s, hn = out_proj(mla_s, ret_s, wo, xs, norm_ffn[0], hn, bm=Bs, bn=1024, hn_off=Rp // Bs)

    a = matmul(hn, w_gate[0], w2=w_up[0], bm=bm_all, bn=256, out_dtype=BF16, name="ffn_gate_up")
    out_p = matmul(a, wd, res=h_p, bm=bm_p, bn=512, m_rows=Rp, name="ffn_down")
    out_s = matmul(a, wd, res=h_s, bm=Bs, bn=512, m_rows=Bs, m_off=Rp // Bs, name="ffn_down")
    y_p = rmsnorm(out_p, norm_final, F32)
    y_s = rmsnorm(out_s, norm_final, F32)

    return (y_p.reshape(B, L, D), y_s.reshape(Bs, T, D),
            c_p.reshape(1, B, L, KV_LORA), kr_p.reshape(1, B, L, ROPE_DIM), st_p[None],
            c_s.reshape(1, Bs, T, KV_LORA), kr_s.reshape(1, Bs, T, ROPE_DIM), st_s[None])
```

```python
import functools
import math

import jax
import jax.numpy as jnp
from jax import lax
from jax.experimental import pallas as pl
from jax.experimental.pallas import tpu as pltpu

F32 = jnp.float32
BF16 = jnp.bfloat16

HEAD_DIM = 128
MLA_HEADS = 16
RET_HEADS = 16
Q_LORA = 1536
KV_LORA = 512
NOPE_DIM = 128
ROPE_DIM = 64
RET_WIDTH = RET_HEADS * HEAD_DIM
QK_PAD = 256
MLA_SCALE = (NOPE_DIM + ROPE_DIM) ** -0.5
ROPE_BASE = 10000.0
RMS_EPS = 1e-6
GN_EPS = 1e-6
NEG_INF = -1e30
PAGE_SIZE = 128
BF16_ROWS = 16

VMEM_LIMIT = 56 * 1024 * 1024


def _params(*sem):
    return pltpu.CompilerParams(dimension_semantics=sem, vmem_limit_bytes=VMEM_LIMIT)


def _row_block(rows, target):
    best = None
    for d in range(BF16_ROWS, min(rows, target) + 1, BF16_ROWS):
        if rows % d == 0:
            best = d
    assert best is not None, rows
    return best


def _rms(x, g):
    ms = jnp.mean(x * x, axis=-1, keepdims=True)
    return x * lax.rsqrt(ms + RMS_EPS) * g


def _silu(x):
    return x * (1.0 / (1.0 + jnp.exp(-x)))


def _dot(a, b):
    return jnp.dot(a, b, preferred_element_type=F32)


def _dot_nt(a, b):
    return lax.dot_general(a, b, (((1,), (1,)), ((), ())), preferred_element_type=F32)


def _dot_tn(a, b):
    return lax.dot_general(a, b, (((0,), (0,)), ((), ())), preferred_element_type=F32)


def _rmsnorm_kernel(x_ref, g_ref, o_ref):
    o_ref[...] = _rms(x_ref[...], g_ref[...]).astype(o_ref.dtype)


def rmsnorm(x, g, out_dtype, br=256):
    R, D = x.shape
    br = min(br, R)
    return pl.pallas_call(
        _rmsnorm_kernel,
        grid=(R // br,),
        in_specs=[pl.BlockSpec((br, D), lambda i: (i, 0)),
                  pl.BlockSpec((1, D), lambda i: (0, 0))],
        out_specs=pl.BlockSpec((br, D), lambda i: (i, 0)),
        out_shape=jax.ShapeDtypeStruct((R, D), out_dtype),
        compiler_params=_params("parallel"),
        name="rmsnorm",
    )(x, g.reshape(1, D))


def _rmsnorm2_kernel(xp_ref, xs_ref, g_ref, o_ref, *, npb):
    i = pl.program_id(0)

    @pl.when(i < npb)
    def _():
        o_ref[...] = _rms(xp_ref[...], g_ref[...]).astype(o_ref.dtype)

    @pl.when(i >= npb)
    def _():
        o_ref[...] = _rms(xs_ref[...], g_ref[...]).astype(o_ref.dtype)


def rmsnorm_rows(xp, xs, g):
    Rp, D = xp.shape
    Rs = xs.shape[0]
    assert Rp % Rs == 0
    npb = Rp // Rs
    return pl.pallas_call(
        functools.partial(_rmsnorm2_kernel, npb=npb),
        grid=(npb + 1,),
        in_specs=[pl.BlockSpec((Rs, D), lambda i: (jnp.minimum(i, npb - 1), 0)),
                  pl.BlockSpec((Rs, D), lambda i: (0, 0)),
                  pl.BlockSpec((1, D), lambda i: (0, 0))],
        out_specs=pl.BlockSpec((Rs, D), lambda i: (i, 0)),
        out_shape=jax.ShapeDtypeStruct((Rp + Rs, D), BF16),
        compiler_params=_params("arbitrary"),
        name="rmsnorm_rows",
    )(xp, xs, g.reshape(1, D))


def _mm_kernel(*refs, mode):
    if mode == "glu":
        x_ref, w_ref, w2_ref, o_ref = refs
        x = x_ref[...]
        g = _dot(x, w_ref[...].astype(BF16))
        u = _dot(x, w2_ref[...].astype(BF16))
        o_ref[...] = (_silu(g) * u).astype(o_ref.dtype)
    elif mode == "res":
        x_ref, w_ref, r_ref, o_ref = refs
        o_ref[...] = (r_ref[...] + _dot(x_ref[...], w_ref[...].astype(BF16))).astype(o_ref.dtype)
    else:
        x_ref, w_ref, o_ref = refs
        o_ref[...] = _dot(x_ref[...], w_ref[...].astype(BF16)).astype(o_ref.dtype)


def matmul(x, w, *, bm, bn, out_dtype=F32, res=None, w2=None, m_rows=None, m_off=0, name="matmul"):
    K = x.shape[1]
    N = w.shape[1]
    M = x.shape[0] if m_rows is None else m_rows
    bn = min(bn, N)
    mode = "glu" if w2 is not None else ("res" if res is not None else "plain")
    assert M % bm == 0 and N % bn == 0 and bn % 128 == 0
    in_specs = [pl.BlockSpec((bm, K), lambda i, j: (i + m_off, 0)),
                pl.BlockSpec((K, bn), lambda i, j: (0, j))]
    args = [x, w]
    if mode == "glu":
        in_specs.append(pl.BlockSpec((K, bn), lambda i, j: (0, j)))
        args.append(w2)
    if mode == "res":
        in_specs.append(pl.BlockSpec((bm, bn), lambda i, j: (i, j)))
        args.append(res)
    return pl.pallas_call(
        functools.partial(_mm_kernel, mode=mode),
        grid=(M // bm, N // bn),
        in_specs=in_specs,
        out_specs=pl.BlockSpec((bm, bn), lambda i, j: (i, j)),
        out_shape=jax.ShapeDtypeStruct((M, N), out_dtype),
        compiler_params=_params("parallel", "parallel"),
        name=name,
    )(*args)


def _mm_nt_kernel(x_ref, wt_ref, o_ref):
    o_ref[...] = _dot_nt(x_ref[...], wt_ref[...])


def matmul_nt(x, wt, *, bm, bn, name):
    M, K = x.shape
    N = wt.shape[0]
    bn = min(bn, N)
    assert wt.shape[1] == K and M % bm == 0 and N % bn == 0
    return pl.pallas_call(
        _mm_nt_kernel,
        grid=(M // bm, N // bn),
        in_specs=[pl.BlockSpec((bm, K), lambda i, j: (i, 0)),
                  pl.BlockSpec((bn, K), lambda i, j: (j, 0))],
        out_specs=pl.BlockSpec((bm, bn), lambda i, j: (i, j)),
        out_shape=jax.ShapeDtypeStruct((M, N), F32),
        compiler_params=_params("parallel", "parallel"),
        name=name,
    )(x, wt)


def _out_proj_kernel(xa_ref, xb_ref, wa_ref, wb_ref, r_ref, g_ref, hn_in_ref, h_ref, hn_ref, row_scr,
                     *, nj, bn):
    del hn_in_ref
    j = pl.program_id(1)
    h = r_ref[...] + (_dot(xa_ref[...], wa_ref[...]) + _dot(xb_ref[...], wb_ref[...]))
    h_ref[...] = h
    row_scr[j] = h

    @pl.when(j == nj - 1)
    def _():
        n_cols = nj * bn
        ss = None
        for jj in range(nj):
            hj = row_scr[jj]
            s = jnp.sum(hj * hj, axis=-1, keepdims=True)
            ss = s if ss is None else ss + s
        inv = lax.rsqrt(ss / n_cols + RMS_EPS)
        for jj in range(nj):
            c = slice(jj * bn, (jj + 1) * bn)
            hn_ref[:, c] = (row_scr[jj] * inv * g_ref[:, c]).astype(hn_ref.dtype)


def out_proj(xa, xb, w, res, g, hn_buf, *, bm, bn, hn_off=0):
    M, Ka = xa.shape
    N = w.shape[1]
    nj = N // bn
    assert xb.shape[1] == Ka and w.shape[0] == 2 * Ka and M % bm == 0 and N % bn == 0
    assert hn_buf.shape[1] == N and hn_buf.dtype == BF16
    return pl.pallas_call(
        functools.partial(_out_proj_kernel, nj=nj, bn=bn),
        grid=(M // bm, nj),
        in_specs=[pl.BlockSpec((bm, Ka), lambda i, j: (i, 0)),
                  pl.BlockSpec((bm, Ka), lambda i, j: (i, 0)),
                  pl.BlockSpec((Ka, bn), lambda i, j: (0, j)),
                  pl.BlockSpec((Ka, bn), lambda i, j: (1, j)),
                  pl.BlockSpec((bm, bn), lambda i, j: (i, j)),
                  pl.BlockSpec((1, N), lambda i, j: (0, 0)),
                  pl.BlockSpec(memory_space=pl.ANY)],
        out_specs=[pl.BlockSpec((bm, bn), lambda i, j: (i, j)),
                   pl.BlockSpec((bm, N), lambda i, j: (i + hn_off, 0))],
        out_shape=[jax.ShapeDtypeStruct((M, N), F32),
                   jax.ShapeDtypeStruct(hn_buf.shape, BF16)],
        scratch_shapes=[pltpu.VMEM((nj, bm, bn), F32)],
        input_output_aliases={6: 1},
        compiler_params=_params("parallel", "arbitrary"),
        name="out_proj",
    )(xa, xb, w, w, res, g.reshape(1, N), hn_buf)


def _rope_tables(pos, half, width):
    inv = jnp.exp(-math.log(ROPE_BASE) * jnp.arange(half, dtype=F32) / half)
    ang = pos.astype(F32)[:, None] * inv[None, :]
    cos, sin = jnp.cos(ang), jnp.sin(ang)
    n = pos.shape[0]
    pad = width - 2 * half
    cos_t = jnp.concatenate([cos, cos, jnp.ones((n, pad), F32)], axis=-1)
    sin_t = jnp.concatenate([-sin, sin, jnp.zeros((n, pad), F32)], axis=-1)
    return cos_t, sin_t


def _rope64_tile(t, cos, sin):
    lane = lax.broadcasted_iota(jnp.int32, t.shape, 1)
    other = jnp.where(lane < ROPE_DIM // 2, pltpu.roll(t, 128 - ROPE_DIM // 2, axis=1),
                      pltpu.roll(t, ROPE_DIM // 2, axis=1))
    return t * cos + other * sin


def _rope128_tile(t, cos, sin):
    return t * cos + pltpu.roll(t, HEAD_DIM // 2, axis=1) * sin


def _mla_prep_kernel(h_ref, kr_ref, cos_ref, sin_ref, gq_ref, gkv_ref, wq_ref, wuk_ref, wuv_ref,
                     q_ref, c_ref, kro_ref, k_ref, v_ref):
    cos, sin = cos_ref[...], sin_ref[...]
    qn = _rms(h_ref[:, :Q_LORA], gq_ref[...]).astype(BF16)
    q = _dot(qn, wq_ref[...])
    for hh in range(MLA_HEADS):
        o = hh * QK_PAD
        q_ref[:, o:o + NOPE_DIM] = q[:, o:o + NOPE_DIM].astype(BF16)
        q_ref[:, o + NOPE_DIM:o + QK_PAD] = _rope64_tile(q[:, o + NOPE_DIM:o + QK_PAD], cos, sin).astype(BF16)
    c = _rms(h_ref[:, Q_LORA:Q_LORA + KV_LORA], gkv_ref[...])
    c_ref[...] = c
    kr = _rope64_tile(kr_ref[...], cos, sin)
    kro_ref[...] = kr[:, :ROPE_DIM]
    cb = c.astype(BF16)
    krb = kr.astype(BF16)
    kn = _dot(cb, wuk_ref[...])
    for hh in range(MLA_HEADS):
        o = hh * QK_PAD
        k_ref[:, o:o + NOPE_DIM] = kn[:, hh * NOPE_DIM:(hh + 1) * NOPE_DIM].astype(BF16)
        k_ref[:, o + NOPE_DIM:o + QK_PAD] = krb
    v_ref[...] = _dot(cb, wuv_ref[...]).astype(BF16)


def mla_prep(h_main, kr_raw, cos_t, sin_t, gq, gkv, wq_pad, wuk2, wuv2, *, bm):
    R = h_main.shape[0]
    assert R % bm == 0
    row = lambda i: (i, 0)
    const = lambda i: (0, 0)
    return pl.pallas_call(
        _mla_prep_kernel,
        grid=(R // bm,),
        in_specs=[pl.BlockSpec((bm, Q_LORA + KV_LORA), row),
                  pl.BlockSpec((bm, 128), row),
                  pl.BlockSpec((bm, 128), row),
                  pl.BlockSpec((bm, 128), row),
                  pl.BlockSpec((1, Q_LORA), const),
                  pl.BlockSpec((1, KV_LORA), const),
                  pl.BlockSpec(wq_pad.shape, const, pipeline_mode=pl.Buffered(1)),
                  pl.BlockSpec(wuk2.shape, const),
                  pl.BlockSpec(wuv2.shape, const)],
        out_specs=[pl.BlockSpec((bm, MLA_HEADS * QK_PAD), row),
                   pl.BlockSpec((bm, KV_LORA), row),
                   pl.BlockSpec((bm, ROPE_DIM), row),
                   pl.BlockSpec((bm, MLA_HEADS * QK_PAD), row),
                   pl.BlockSpec((bm, MLA_HEADS * HEAD_DIM), row)],
        out_shape=[jax.ShapeDtypeStruct((R, MLA_HEADS * QK_PAD), BF16),
                   jax.ShapeDtypeStruct((R, KV_LORA), F32),
                   jax.ShapeDtypeStruct((R, ROPE_DIM), F32),
                   jax.ShapeDtypeStruct((R, MLA_HEADS * QK_PAD), BF16),
                   jax.ShapeDtypeStruct((R, MLA_HEADS * HEAD_DIM), BF16)],
        compiler_params=_params("parallel"),
        name="mla_prep",
    )(h_main, kr_raw, cos_t, sin_t, gq.reshape(1, -1), gkv.reshape(1, -1), wq_pad, wuk2, wuv2)


def _causal_attn_kernel(q_ref, k_ref, v_ref, o_ref, *, L, bq):
    tri = (lax.broadcasted_iota(jnp.int32, (bq, bq), 0) >= lax.broadcasted_iota(jnp.int32, (bq, bq), 1))
    for qi in range(L // bq):
        lo, hi = qi * bq, (qi + 1) * bq
        q = q_ref[lo:hi, :]
        s_d = jnp.where(tri, _dot_nt(q, k_ref[lo:hi, :]) * MLA_SCALE, NEG_INF)
        m = jnp.max(s_d, axis=-1, keepdims=True)
        if qi > 0:
            s_p = _dot_nt(q, k_ref[:lo, :]) * MLA_SCALE
            m = jnp.maximum(m, jnp.max(s_p, axis=-1, keepdims=True))
        p_d = jnp.exp(s_d - m)
        l = jnp.sum(p_d, axis=-1, keepdims=True)
        acc = _dot(p_d.astype(BF16), v_ref[lo:hi, :])
        if qi > 0:
            p_p = jnp.exp(s_p - m)
            l = l + jnp.sum(p_p, axis=-1, keepdims=True)
            acc = acc + _dot(p_p.astype(BF16), v_ref[:lo, :])
        o_ref[lo:hi, :] = (acc / l).astype(o_ref.dtype)


def mla_prompt_attention(Q, K, V, B, L, *, bq=512):
    bq = min(bq, L)
    return pl.pallas_call(
        functools.partial(_causal_attn_kernel, L=L, bq=bq),
        grid=(B, MLA_HEADS),
        in_specs=[pl.BlockSpec((L, QK_PAD), lambda b, h: (b, h)),
                  pl.BlockSpec((L, QK_PAD), lambda b, h: (b, h)),
                  pl.BlockSpec((L, HEAD_DIM), lambda b, h: (b, h))],
        out_specs=pl.BlockSpec((L, HEAD_DIM), lambda b, h: (b, h)),
        out_shape=jax.ShapeDtypeStruct((B * L, MLA_HEADS * HEAD_DIM), BF16),
        compiler_params=_params("parallel", "parallel"),
        name="mla_prompt_attention",
    )(Q, K, V)


def _group_norm_gate(o, rg, gw):
    mu = jnp.mean(o, axis=-1, keepdims=True)
    d = o - mu
    var = jnp.mean(d * d, axis=-1, keepdims=True)
    return _silu(rg) * (d * lax.rsqrt(var + GN_EPS) * gw)


def _ret_prompt_kernel(rq_ref, rk_ref, rv_ref, rg_ref, cos_ref, sin_ref, gw_ref, lg_ref,
                       o_ref, s_ref, *, L, C):
    lg = lg_ref[0, :, :1]
    ii = lax.broadcasted_iota(jnp.int32, (C, C), 0)
    jj = lax.broadcasted_iota(jnp.int32, (C, C), 1)
    diff = (ii - jj).astype(F32)
    dec = jnp.where(diff >= 0, jnp.exp(jnp.maximum(diff, 0.0) * lg), 0.0)
    idx = lax.broadcasted_iota(jnp.int32, (C, 1), 0).astype(F32)
    q_dec = jnp.exp((idx + 1.0) * lg)
    k_dec = jnp.exp((C - 1.0 - idx) * lg)
    s_dec = jnp.exp(C * lg)
    gw = gw_ref[...]
    S = jnp.zeros((HEAD_DIM, HEAD_DIM), F32)
    for n in range(L // C):
        r = slice(n * C, (n + 1) * C)
        cos, sin = cos_ref[r, :], sin_ref[r, :]
        qc = _rope128_tile(rq_ref[r, :], cos, sin)
        kc = _rope128_tile(rk_ref[r, :], cos, sin) * (HEAD_DIM ** -0.5)
        qb = qc.astype(BF16)
        vb = rv_ref[r, :].astype(BF16)
        sc = _dot_nt(qb, kc.astype(BF16)) * dec
        o = _dot(sc.astype(BF16), vb) + _dot(qb, S.astype(BF16)) * q_dec
        S = S * s_dec + _dot_tn((kc * k_dec).astype(BF16), vb)
        o_ref[r, :] = _group_norm_gate(o, rg_ref[r, :], gw).astype(o_ref.dtype)
    s_ref[...] = S


def retention_prompt(h_main, cos_t, sin_t, gn_w, lg_tab, B, L, *, C=256):
    C = min(C, L)
    nb = RET_WIDTH // HEAD_DIM
    base = (Q_LORA + KV_LORA) // HEAD_DIM
    blk = lambda off: pl.BlockSpec((L, HEAD_DIM), lambda b, h, off=off: (b, off + h))
    return pl.pallas_call(
        functools.partial(_ret_prompt_kernel, L=L, C=C),
        grid=(B, RET_HEADS),
        in_specs=[blk(base), blk(base + nb), blk(base + 2 * nb), blk(base + 3 * nb),
                  pl.BlockSpec((L, HEAD_DIM), lambda b, h: (0, 0)),
                  pl.BlockSpec((L, HEAD_DIM), lambda b, h: (0, 0)),
                  pl.BlockSpec((1, HEAD_DIM), lambda b, h: (0, h)),
                  pl.BlockSpec((1, 1, HEAD_DIM), lambda b, h: (h, 0, 0))],
        out_specs=[pl.BlockSpec((L, HEAD_DIM), lambda b, h: (b, h)),
                   pl.BlockSpec((None, None, HEAD_DIM, HEAD_DIM), lambda b, h: (b, h, 0, 0))],
        out_shape=[jax.ShapeDtypeStruct((B * L, RET_WIDTH), BF16),
                   jax.ShapeDtypeStruct((B, RET_HEADS, HEAD_DIM, HEAD_DIM), F32)],
        compiler_params=_params("parallel", "parallel"),
        name="retention_prompt",
    )(h_main, h_main, h_main, h_main, cos_t, sin_t, gn_w.reshape(1, -1), lg_tab)


def _ret_sample_kernel(rq_ref, rk_ref, rv_ref, rg_ref, cos_ref, sin_ref, gw_ref, lg_ref, s0_ref,
                       o_ref, s_ref, q_scr, k_scr, v_scr, oc_scr, *, bb):
    gamma = jnp.exp(lg_ref[0, :, :1])
    cos, sin = cos_ref[...], sin_ref[...]
    q = _rope128_tile(rq_ref[...], cos, sin)
    k = _rope128_tile(rk_ref[...], cos, sin) * (HEAD_DIM ** -0.5)
    v = rv_ref[...]
    q_scr[...] = q
    k_scr[...] = k
    v_scr[...] = v
    eye = (lax.broadcasted_iota(jnp.int32, (HEAD_DIM, HEAD_DIM), 0)
           == lax.broadcasted_iota(jnp.int32, (HEAD_DIM, HEAD_DIM), 1))

    def body(b, carry):
        s0 = s0_ref[b]
        qb = jnp.broadcast_to(q_scr[pl.ds(b, 1), :], (8, HEAD_DIM)).astype(BF16)
        oc_scr[pl.ds(b, 1), :] = _dot(qb, s0.astype(BF16))[:1, :]
        kd = jnp.where(eye, jnp.broadcast_to(k_scr[pl.ds(b, 1), :], (HEAD_DIM, HEAD_DIM)), 0.0)
        vr = jnp.broadcast_to(v_scr[pl.ds(b, 1), :], (HEAD_DIM, HEAD_DIM))
        s_ref[b] = s0 * gamma + _dot(kd.astype(BF16), vr.astype(BF16))
        return carry

    lax.fori_loop(0, bb, body, 0, unroll=min(8, bb))
    qk = jnp.sum(q * k, axis=-1, keepdims=True)
    o = qk * v + oc_scr[...] * gamma
    o_ref[...] = _group_norm_gate(o, rg_ref[...], gw_ref[...]).astype(o_ref.dtype)


def retention_sample(h_main, cos_t, sin_t, gn_w, lg_tab, s0, *, row0, bb=32):
    Bs = s0.shape[0]
    bb = min(bb, Bs)
    assert Bs % bb == 0 and row0 % bb == 0
    r0 = row0 // bb
    nb = RET_WIDTH // HEAD_DIM
    base = (Q_LORA + KV_LORA) // HEAD_DIM
    blk = lambda off: pl.BlockSpec((bb, HEAD_DIM), lambda h, i, off=off: (r0 + i, off + h))
    return pl.pallas_call(
        functools.partial(_ret_sample_kernel, bb=bb),
        grid=(RET_HEADS, Bs // bb),
        in_specs=[blk(base), blk(base + nb), blk(base + 2 * nb), blk(base + 3 * nb),
                  pl.BlockSpec((bb, HEAD_DIM), lambda h, i: (i, 0)),
                  pl.BlockSpec((bb, HEAD_DIM), lambda h, i: (i, 0)),
                  pl.BlockSpec((1, HEAD_DIM), lambda h, i: (0, h)),
                  pl.BlockSpec((1, 1, HEAD_DIM), lambda h, i: (h, 0, 0)),
                  pl.BlockSpec((bb, None, HEAD_DIM, HEAD_DIM), lambda h, i: (i, h, 0, 0))],
        out_specs=[pl.BlockSpec((bb, HEAD_DIM), lambda h, i: (i, h)),
                   pl.BlockSpec((bb, None, HEAD_DIM, HEAD_DIM), lambda h, i: (i, h, 0, 0))],
        out_shape=[jax.ShapeDtypeStruct((Bs, RET_WIDTH), BF16),
                   jax.ShapeDtypeStruct(s0.shape, F32)],
        scratch_shapes=[pltpu.VMEM((bb, HEAD_DIM), F32)] * 4,
        compiler_params=_params("parallel", "parallel"),
        name="retention_sample",
    )(h_main, h_main, h_main, h_main, cos_t, sin_t, gn_w.reshape(1, -1), lg_tab, s0)


def _q_latent_kernel(q_ref, w_ref, o_ref):
    o_ref[...] = _dot_nt(q_ref[:, :NOPE_DIM], w_ref[...]).astype(o_ref.dtype)


def q_latent(Q, wuk_heads, *, row0, Bs):
    assert row0 % Bs == 0
    r0 = row0 // Bs
    return pl.pallas_call(
        _q_latent_kernel,
        grid=(MLA_HEADS,),
        in_specs=[pl.BlockSpec((Bs, QK_PAD), lambda h: (r0, h)),
                  pl.BlockSpec((None, KV_LORA, NOPE_DIM), lambda h: (h, 0, 0))],
        out_specs=pl.BlockSpec((Bs, KV_LORA), lambda h: (0, h)),
        out_shape=jax.ShapeDtypeStruct((Bs, MLA_HEADS * KV_LORA), BF16),
        compiler_params=_params("parallel"),
        name="q_latent",
    )(Q, wuk_heads)


PAGED_SLOTS = 4


def _paged_attn_kernel(pt_ref, ql_ref, qr_ref, cn_ref, rn_ref, cache_c, cache_r, o_ref,
                       cbuf, rbuf, sem, *, G, NC, Bs):
    b = pl.program_id(0)
    D = PAGED_SLOTS - 1

    def chunk_copies(page_ids, slot):
        cps = []
        for g, pid in enumerate(page_ids):
            rows = pl.ds(g * PAGE_SIZE, PAGE_SIZE)
            cps.append(pltpu.make_async_copy(cache_c.at[pid], cbuf.at[slot, rows, :], sem.at[0, slot]))
            cps.append(pltpu.make_async_copy(cache_r.at[pid], rbuf.at[slot, :, rows], sem.at[1, slot]))
        return cps

    def start_chunk(bb, c):
        ids = [pt_ref[bb, c * G + g] for g in range(G)]
        for cp in chunk_copies(ids, c % PAGED_SLOTS):
            cp.start()

    def wait_chunk(c):
        for cp in chunk_copies([0] * G, c % PAGED_SLOTS):
            cp.wait()

    @pl.when(b == 0)
    def _():
        for c in range(D):
            start_chunk(0, c)

    ql = ql_ref[...]
    qr = qr_ref[...]
    m = jnp.full((MLA_HEADS, 1), NEG_INF, F32)
    l = jnp.zeros((MLA_HEADS, 1), F32)
    acc = jnp.zeros((MLA_HEADS, KV_LORA), F32)
    for c in range(NC):
        nxt = c + D
        if nxt < NC:
            start_chunk(b, nxt)
        else:
            @pl.when(b + 1 < Bs)
            def _(nxt=nxt):
                start_chunk(b + 1, nxt - NC)
        wait_chunk(c)
        slot = c % PAGED_SLOTS
        cp = cbuf[slot].astype(BF16)
        s = (_dot_nt(ql, cp) + _dot(qr, rbuf[slot].astype(BF16))) * MLA_SCALE
        m_new = jnp.maximum(m, jnp.max(s, axis=-1, keepdims=True))
        alpha = jnp.exp(m - m_new)
        p = jnp.exp(s - m_new)
        l = l * alpha + jnp.sum(p, axis=-1, keepdims=True)
        acc = acc * alpha + _dot(p.astype(BF16), cp)
        m = m_new

    cn = cn_ref[...]
    s = (jnp.sum(ql.astype(F32) * cn, axis=-1, keepdims=True)
         + jnp.sum(qr.astype(F32) * rn_ref[...], axis=-1, keepdims=True)) * MLA_SCALE
    m2 = jnp.maximum(m, s)
    a2 = jnp.exp(m - m2)
    p = jnp.exp(s - m2)
    o_ref[...] = ((acc * a2 + p * cn) / (l * a2 + p)).astype(o_ref.dtype)


def paged_attention(page_table, q_lat, q_rope, c_new, kr_new, cache_c, cache_rt, *, G=16):
    Bs, n_pages = page_table.shape
    G = min(G, n_pages // PAGED_SLOTS)
    NC = n_pages // G
    assert n_pages % G == 0 and NC % PAGED_SLOTS == 0
    per_b = lambda d1, d2: pl.BlockSpec((None, d1, d2), lambda b, pt: (b, 0, 0))
    grid_spec = pltpu.PrefetchScalarGridSpec(
        num_scalar_prefetch=1,
        grid=(Bs,),
        in_specs=[per_b(MLA_HEADS, KV_LORA), per_b(MLA_HEADS, ROPE_DIM), per_b(1, KV_LORA), per_b(1, ROPE_DIM),
                  pl.BlockSpec(memory_space=pl.ANY), pl.BlockSpec(memory_space=pl.ANY)],
        out_specs=per_b(MLA_HEADS, KV_LORA),
        scratch_shapes=[pltpu.VMEM((PAGED_SLOTS, G * PAGE_SIZE, KV_LORA), F32),
                        pltpu.VMEM((PAGED_SLOTS, ROPE_DIM, G * PAGE_SIZE), F32),
                        pltpu.SemaphoreType.DMA((2, PAGED_SLOTS))],
    )
    return pl.pallas_call(
        functools.partial(_paged_attn_kernel, G=G, NC=NC, Bs=Bs),
        grid_spec=grid_spec,
        out_shape=jax.ShapeDtypeStruct((Bs, MLA_HEADS, KV_LORA), BF16),
        compiler_params=_params("arbitrary"),
        name="paged_attention",
    )(page_table, q_lat, q_rope, c_new.reshape(Bs, 1, KV_LORA), kr_new.reshape(Bs, 1, ROPE_DIM),
      cache_c, cache_rt)


def _v_up_kernel(o_ref, w_ref, out_ref):
    out_ref[...] = _dot(o_ref[...], w_ref[...]).astype(out_ref.dtype)


def v_up(o_lat2, wuv_heads):
    Bs = o_lat2.shape[0]
    return pl.pallas_call(
        _v_up_kernel,
        grid=(MLA_HEADS,),
        in_specs=[pl.BlockSpec((Bs, KV_LORA), lambda h: (0, h)),
                  pl.BlockSpec((None, KV_LORA, HEAD_DIM), lambda h: (h, 0, 0))],
        out_specs=pl.BlockSpec((Bs, HEAD_DIM), lambda h: (0, h)),
        out_shape=jax.ShapeDtypeStruct((Bs, MLA_HEADS * HEAD_DIM), BF16),
        compiler_params=_params("parallel"),
        name="v_up",
    )(o_lat2, wuv_heads)


REPACK_BN = 512


def _repack_kernel(a_ref, b_ref, o_ref, *, first_shifted):
    j = pl.program_id(0)

    @pl.when(j < first_shifted)
    def _():
        o_ref[...] = a_ref[...].astype(BF16)

    @pl.when(j >= first_shifted)
    def _():
        o_ref[:REPACK_BN - ROPE_DIM, :] = a_ref[ROPE_DIM:, :].astype(BF16)
        o_ref[REPACK_BN - ROPE_DIM:, :] = b_ref[...].astype(BF16)


def repack_w_in_t(w_in_t):
    n_in, K = w_in_t.shape
    c0 = Q_LORA + KV_LORA
    n_out = n_in - ROPE_DIM
    assert c0 % REPACK_BN == 0 and n_out % REPACK_BN == 0 and REPACK_BN % ROPE_DIM == 0
    per = REPACK_BN // ROPE_DIM
    return pl.pallas_call(
        functools.partial(_repack_kernel, first_shifted=c0 // REPACK_BN),
        grid=(n_out // REPACK_BN,),
        in_specs=[pl.BlockSpec((REPACK_BN, K), lambda j: (j, 0)),
                  pl.BlockSpec((ROPE_DIM, K), lambda j: ((j + 1) * per, 0))],
        out_specs=pl.BlockSpec((REPACK_BN, K), lambda j: (j, 0)),
        out_shape=jax.ShapeDtypeStruct((n_out, K), BF16),
        compiler_params=_params("parallel"),
        name="repack_w_in",
    )(w_in_t, w_in_t)


def _prep_weights(w_in, w_q_up, w_uk, w_uv, w_o, w_down):
    c0 = Q_LORA + KV_LORA
    w_in_t = jnp.swapaxes(w_in, 0, 1)
    w_main = repack_w_in_t(w_in_t)
    w_kr = jnp.pad(w_in_t[c0:c0 + ROPE_DIM], ((0, 128 - ROPE_DIM), (0, 0))).astype(BF16)
    wq = w_q_up.reshape(Q_LORA, MLA_HEADS, NOPE_DIM + ROPE_DIM)
    wq_pad = jnp.pad(wq, ((0, 0), (0, 0), (0, QK_PAD - NOPE_DIM - ROPE_DIM))).reshape(Q_LORA, MLA_HEADS * QK_PAD).astype(BF16)
    wuk2 = w_uk.reshape(KV_LORA, MLA_HEADS * NOPE_DIM).astype(BF16)
    wuv2 = w_uv.reshape(KV_LORA, MLA_HEADS * HEAD_DIM).astype(BF16)
    wuk_h = jnp.transpose(w_uk, (1, 0, 2)).astype(BF16)
    wuv_h = jnp.transpose(w_uv, (1, 0, 2)).astype(BF16)
    return w_main, w_kr, wq_pad, wuk2, wuv2, wuk_h, wuv_h, w_o.astype(BF16), w_down.astype(BF16)


def kernel(x_prompt, x_sample, cache_kv_latent, cache_k_rope, state_retention, page_table,
           w_in, norm_attn, q_a_norm, w_q_up, kv_a_norm, w_uk, w_uv, ret_gn_w, w_o,
           norm_ffn, w_gate, w_up, w_down, norm_final):
    B, L, D = x_prompt.shape
    Bs, T, _ = x_sample.shape
    assert w_in.shape[0] == 1 and T == 1
    Rp = B * L
    R = Rp + Bs
    past = page_table.shape[1] * PAGE_SIZE
    (w_main, w_kr, wq_pad, wuk2, wuv2, wuk_h, wuv_h, wo, wd) = _prep_weights(
        w_in[0], w_q_up[0], w_uk[0], w_uv[0], w_o[0], w_down[0])
    lg = jnp.log1p(-jnp.exp2(-5.0 - jnp.arange(RET_HEADS, dtype=F32)))
    lg_tab = jnp.broadcast_to(lg[:, None, None], (RET_HEADS, 1, HEAD_DIM))

    pos_p = jnp.arange(L, dtype=jnp.int32)
    pos_s = jnp.full((Bs,), past, dtype=jnp.int32)
    cos64_p, sin64_p = _rope_tables(pos_p, ROPE_DIM // 2, 128)
    cos64_s, sin64_s = _rope_tables(pos_s, ROPE_DIM // 2, 128)
    cos64 = jnp.concatenate([jnp.tile(cos64_p, (B, 1)), cos64_s], axis=0)
    sin64 = jnp.concatenate([jnp.tile(sin64_p, (B, 1)), sin64_s], axis=0)
    cos128_p, sin128_p = _rope_tables(pos_p, HEAD_DIM // 2, 128)
    cos128_s, sin128_s = _rope_tables(pos_s, HEAD_DIM // 2, 128)

    xp = x_prompt.reshape(Rp, D)
    xs = x_sample.reshape(Bs, D)
    bm_all = _row_block(R, R // 8)
    bm_p = _row_block(Rp, 512)

    xn = rmsnorm_rows(xp, xs, norm_attn[0])
    h_main = matmul_nt(xn, w_main, bm=bm_all, bn=512, name="in_proj")
    kr_raw = matmul_nt(xn, w_kr, bm=bm_all, bn=128, name="in_proj_kr")
    Q, c_all, kr_all, K, V = mla_prep(h_main, kr_raw, cos64, sin64, q_a_norm[0], kv_a_norm[0],
                                      wq_pad, wuk2, wuv2, bm=_row_block(R, 320))
    c_p, c_s = c_all[:Rp], c_all[Rp:]
    kr_p, kr_s = kr_all[:Rp], kr_all[Rp:]

    mla_p = mla_prompt_attention(Q, K, V, B, L)
    ret_p, st_p = retention_prompt(h_main, cos128_p, sin128_p, ret_gn_w[0], lg_tab, B, L)

    q_lat = q_latent(Q, wuk_h, row0=Rp, Bs=Bs).reshape(Bs, MLA_HEADS, KV_LORA)
    q_rope = Q[Rp:].reshape(Bs, MLA_HEADS, QK_PAD)[:, :, NOPE_DIM:NOPE_DIM + ROPE_DIM]
    cache_rt = jnp.swapaxes(cache_k_rope[0], 1, 2)
    o_lat = paged_attention(page_table, q_lat, q_rope, c_s, kr_s, cache_kv_latent[0], cache_rt)
    mla_s = v_up(o_lat.reshape(Bs, MLA_HEADS * KV_LORA), wuv_h)
    ret_s, st_s = retention_sample(h_main, cos128_s, sin128_s, ret_gn_w[0], lg_tab, state_retention[0], row0=Rp)

    h_p, hn = out_proj(mla_p, ret_p, wo, xp, norm_ffn[0], xn, bm=bm_p, bn=1024)
    h_s, hn = out_proj(mla_s, ret_s, wo, xs, norm_ffn[0], hn, bm=Bs, bn=1024, hn_off=Rp // Bs)

    a = matmul(hn, w_gate[0], w2=w_up[0], bm=bm_all, bn=256, out_dtype=BF16, name="ffn_gate_up")
    out_p = matmul(a, wd, res=h_p, bm=bm_p, bn=512, m_rows=Rp, name="ffn_down")
    out_s = matmul(a, wd, res=h_s, bm=Bs, bn=512, m_rows=Bs, m_off=Rp // Bs, name="ffn_down")
    y_p = rmsnorm(out_p, norm_final, F32)
    y_s = rmsnorm(out_s, norm_final, F32)

    return (y_p.reshape(B, L, D), y_s.reshape(Bs, T, D),
            c_p.reshape(1, B, L, KV_LORA), kr_p.reshape(1, B, L, ROPE_DIM), st_p[None],
            c_s.reshape(1, Bs, T, KV_LORA), kr_s.reshape(1, Bs, T, ROPE_DIM), st_s[None])
```
